```python
import jax, jax.numpy as jnp
from jax import lax
import numpy as np

D_MODEL = 1024
BATCH = 8
SEQ = 4096
DEPTH = 2

A_WIDTH = D_MODEL // 2
A_GROUPS = 4
A_GROUP_DIM = A_WIDTH // A_GROUPS
CHUNK = 128
B_WIDTH = D_MODEL // 2
B_HEAD_DIM = 64
B_HEADS = B_WIDTH // B_HEAD_DIM
DILATED_PATTERNS = ((128, 1), (512, 4), (2048, 16))
Q_BLOCK = 128
ROPE_THETA = 10000.0
AB_IN = 2 * A_WIDTH + 3 * B_WIDTH
CONV_WIDTH = 31
FFN_DIM = 2816
FFN_CONV_WIDTH = 3
EPS = 1e-6
NEG = -1e30
N_EVEN = (DEPTH + 1) // 2
N_ODD = DEPTH // 2

kernel_name = "hybrid_gmlp_dilated_conformer_convffn"


def rms_norm(x, g):
    xf = x.astype(jnp.float32)
    y = xf * lax.rsqrt(jnp.mean(xf * xf, axis=-1, keepdims=True) + EPS)
    return (y * g.astype(jnp.float32)).astype(x.dtype)


def layer_norm(x, g):
    xf = x.astype(jnp.float32)
    mu = jnp.mean(xf, axis=-1, keepdims=True)
    var = jnp.mean(jnp.square(xf - mu), axis=-1, keepdims=True)
    return ((xf - mu) * lax.rsqrt(var + EPS) * g.astype(jnp.float32)).astype(x.dtype)


def causal_depthwise_conv(x, w, b):
    k = w.shape[0]
    y = lax.conv_general_dilated(
        x, w[:, None, :].astype(x.dtype), window_strides=(1,), padding=((k - 1, 0),),
        dimension_numbers=("NWC", "WIO", "NWC"), feature_group_count=x.shape[-1])
    return y + b


def rotary(x, positions):
    e = x.shape[-1]
    inv_freq = 1.0 / (ROPE_THETA ** (jnp.arange(0, e, 2, dtype=jnp.float32) / e))
    ang = positions.astype(jnp.float32)[..., None] * inv_freq
    cos, sin = jnp.cos(ang)[:, :, None, :], jnp.sin(ang)[:, :, None, :]
    xf = x.astype(jnp.float32)
    x1, x2 = xf[..., : e // 2], xf[..., e // 2:]
    return jnp.concatenate([x1 * cos - x2 * sin, x2 * cos + x1 * sin], -1).astype(x.dtype)


def dilated_branch(q, k, v, window, dilation):
    bn, s, h, e = q.shape
    L = s // dilation
    w = window // dilation
    qb_len = min(Q_BLOCK, L)
    nb = L // qb_len

    def to_sub(t):
        return t.reshape(bn, L, dilation, h, e).transpose(0, 2, 3, 1, 4)

    qs, ks, vs = to_sub(q), to_sub(k), to_sub(v)
    pad = ((0, 0), (0, 0), (0, 0), (w, 0), (0, 0))
    kp, vp = jnp.pad(ks, pad), jnp.pad(vs, pad)
    starts = jnp.arange(nb) * qb_len
    j = jnp.arange(w + qb_len)
    idx = starts[:, None] + j[None, :]
    kb = jnp.take(kp, idx, axis=3)
    vb = jnp.take(vp, idx, axis=3)
    qb = qs.reshape(bn, dilation, h, nb, qb_len, e)
    sc = jnp.einsum("brhnqe,brhnke->brhnqk", qb, kb).astype(jnp.float32) * (e ** -0.5)
    i = jnp.arange(qb_len)
    dist = i[:, None] + w - j[None, :]
    kpos = starts[:, None, None] + j[None, None, :] - w
    mask = (dist >= 0)[None] & (dist <= w)[None] & (kpos >= 0)
    sc = jnp.where(mask, sc, NEG)
    m = jnp.max(sc, axis=-1, keepdims=True)
    p = jnp.exp(sc - m)
    den = jnp.sum(p, axis=-1, keepdims=True)
    o = jnp.einsum("brhnqk,brhnke->brhnqe", p, vb.astype(jnp.float32)) / den
    lse = (m + jnp.log(den))[..., 0]
    o = o.reshape(bn, dilation, h, L, e).transpose(0, 3, 1, 2, 4).reshape(bn, s, h, e)
    lse = lse.reshape(bn, dilation, h, L).transpose(0, 3, 1, 2).reshape(bn, s, h)
    return o, lse


def mixer_ab(h, positions, w_in, a_vnorm_g, a_spatial_w, a_spatial_b, q_norm_g, k_norm_g, w_out):
    bn, s, _ = h.shape
    z = h @ w_in
    ua, va, q, k, v = jnp.split(
        z, [A_WIDTH, 2 * A_WIDTH, 2 * A_WIDTH + B_WIDTH, 2 * A_WIDTH + 2 * B_WIDTH], axis=-1)
    nc = s // CHUNK
    ua = jax.nn.gelu(ua, approximate=False).reshape(bn, nc, CHUNK, A_GROUPS, A_GROUP_DIM)
    va = layer_norm(jax.nn.gelu(va, approximate=False).reshape(bn, s, A_GROUPS, A_GROUP_DIM), a_vnorm_g)
    va = va.reshape(bn, nc, CHUNK, A_GROUPS, A_GROUP_DIM)
    causal = jnp.tril(jnp.ones((CHUNK, CHUNK), dtype=bool))
    ws = jnp.where(causal[None], a_spatial_w, 0.0).astype(va.dtype)
    f = jnp.einsum("gts,bcsgd->bctgd", ws, va) + a_spatial_b.T[None, None, :, :, None]
    ya = (ua * f).reshape(bn, s, A_WIDTH)
    q = rotary(rms_norm(q.reshape(bn, s, B_HEADS, B_HEAD_DIM), q_norm_g), positions)
    k = rotary(rms_norm(k.reshape(bn, s, B_HEADS, B_HEAD_DIM), k_norm_g), positions)
    v = v.reshape(bn, s, B_HEADS, B_HEAD_DIM)
    outs, lses = [], []
    for window, dilation in DILATED_PATTERNS:
        o, lse = dilated_branch(q, k, v, window, dilation)
        outs.append(o)
        lses.append(lse)
    wts = jax.nn.softmax(jnp.stack(lses, 0), axis=0)
    yb = jnp.sum(wts[..., None] * jnp.stack(outs, 0), axis=0)
    yb = yb.astype(h.dtype).reshape(bn, s, B_WIDTH)
    return jnp.concatenate([ya, yb], axis=-1) @ w_out


def conformer_conv(h, pw1_w, pw1_b, dw_w, dw_b, ln_g, ln_b, pw2_w, pw2_b):
    a, g = jnp.split(h @ pw1_w + pw1_b, 2, axis=-1)
    y = a * jax.nn.sigmoid(g)
    y = causal_depthwise_conv(y, dw_w, dw_b)
    y = jax.nn.silu(layer_norm(y, ln_g) + ln_b)
    return y @ pw2_w + pw2_b


def conv_ffn(h, up_w, dw_w, dw_b, down_w):
    z = causal_depthwise_conv(h @ up_w, dw_w, dw_b)
    a, b = jnp.split(z, 2, axis=-1)
    return (jax.nn.silu(a) * b) @ down_w


def modulate(x, g, shift, scale):
    return rms_norm(x, g) * (1.0 + scale[:, None, :]) + shift[:, None, :]


def setup_inputs(seed: int = 0) -> dict:
    key = jax.random.key(seed)
    ks = jax.random.split(key, 32)
    D, F = D_MODEL, FFN_DIM

    def nrm(k, shape, scale):
        return jax.random.normal(k, shape, jnp.float32) * scale

    start = jax.random.randint(ks[2], (BATCH, 1), 0, 1024, dtype=jnp.int32)
    positions = (start + jnp.arange(SEQ, dtype=jnp.int32)[None, :]).astype(jnp.int32)
    return {
        "x": nrm(ks[0], (BATCH, SEQ, D), 1.0),
        "c": nrm(ks[1], (BATCH, D), 1.0),
        "positions": positions,
        "ada_w": nrm(ks[3], (DEPTH, D, 6 * D), 0.5 * D ** -0.5),
        "ada_b": nrm(ks[4], (DEPTH, 6 * D), 0.01),
        "norm_mix_g": 1.0 + nrm(ks[5], (DEPTH, D), 0.02),
        "norm_ffn_g": 1.0 + nrm(ks[6], (DEPTH, D), 0.02),
        "ab_w_in": nrm(ks[7], (N_EVEN, D, AB_IN), D ** -0.5),
        "a_vnorm_g": 1.0 + nrm(ks[8], (N_EVEN, A_GROUPS, A_GROUP_DIM), 0.02),
        "a_spatial_w": nrm(ks[9], (N_EVEN, A_GROUPS, CHUNK, CHUNK), CHUNK ** -0.5),
        "a_spatial_b": 1.0 + nrm(ks[10], (N_EVEN, A_GROUPS, CHUNK), 0.1),
        "b_q_norm_g": 1.0 + nrm(ks[11], (N_EVEN, B_HEAD_DIM), 0.02),
        "b_k_norm_g": 1.0 + nrm(ks[12], (N_EVEN, B_HEAD_DIM), 0.02),
        "ab_w_out": nrm(ks[13], (N_EVEN, A_WIDTH + B_WIDTH, D), (A_WIDTH + B_WIDTH) ** -0.5),
        "conv_pw1_w": nrm(ks[14], (N_ODD, D, 2 * D), D ** -0.5),
        "conv_pw1_b": nrm(ks[15], (N_ODD, 2 * D), 0.01),
        "conv_dw_w": nrm(ks[16], (N_ODD, CONV_WIDTH, D), CONV_WIDTH ** -0.5),
        "conv_dw_b": nrm(ks[17], (N_ODD, D), 0.01),
        "conv_ln_g": 1.0 + nrm(ks[18], (N_ODD, D), 0.02),
        "conv_ln_b": nrm(ks[19], (N_ODD, D), 0.01),
        "conv_pw2_w": nrm(ks[20], (N_ODD, D, D), D ** -0.5),
        "conv_pw2_b": nrm(ks[21], (N_ODD, D), 0.01),
        "ffn_up_w": nrm(ks[22], (DEPTH, D, 2 * F), D ** -0.5),
        "ffn_dw_w": nrm(ks[23], (DEPTH, FFN_CONV_WIDTH, 2 * F), FFN_CONV_WIDTH ** -0.5),
        "ffn_dw_b": nrm(ks[24], (DEPTH, 2 * F), 0.01),
        "ffn_down_w": nrm(ks[25], (DEPTH, F, D), F ** -0.5),
    }


def reference(x, c, positions, ada_w, ada_b, norm_mix_g, norm_ffn_g, ab_w_in, a_vnorm_g,
              a_spatial_w, a_spatial_b, b_q_norm_g, b_k_norm_g, ab_w_out, conv_pw1_w,
              conv_pw1_b, conv_dw_w, conv_dw_b, conv_ln_g, conv_ln_b, conv_pw2_w, conv_pw2_b,
              ffn_up_w, ffn_dw_w, ffn_dw_b, ffn_down_w):
    c_act = jax.nn.silu(c)
    for layer in range(DEPTH):
        mod = c_act @ ada_w[layer] + ada_b[layer]
        sh_m, sc_m, g_m, sh_f, sc_f, g_f = jnp.split(mod, 6, axis=-1)
        h = modulate(x, norm_mix_g[layer], sh_m, sc_m)
        li = layer // 2
        if layer % 2 == 0:
            y = mixer_ab(h, positions, ab_w_in[li], a_vnorm_g[li], a_spatial_w[li],
                         a_spatial_b[li], b_q_norm_g[li], b_k_norm_g[li], ab_w_out[li])
        else:
            y = conformer_conv(h, conv_pw1_w[li], conv_pw1_b[li], conv_dw_w[li], conv_dw_b[li],
                               conv_ln_g[li], conv_ln_b[li], conv_pw2_w[li], conv_pw2_b[li])
        x = x + g_m[:, None, :] * y
        h = modulate(x, norm_ffn_g[layer], sh_f, sc_f)
        x = x + g_f[:, None, :] * conv_ffn(h, ffn_up_w[layer], ffn_dw_w[layer],
                                           ffn_dw_b[layer], ffn_down_w[layer])
    return x
```

```python
import functools
import math

import numpy as np
import jax
import jax.numpy as jnp
from jax import lax
from jax.experimental import pallas as pl
from jax.experimental.pallas import tpu as pltpu

F32 = jnp.float32
BF16 = jnp.bfloat16

D_MODEL = 1024
A_WIDTH = 512
A_GROUPS = 4
A_GROUP_DIM = 128
CHUNK = 128
B_WIDTH = 512
B_HEAD_DIM = 64
B_HEADS = 8
HEADS_PER_GROUP = 4
GROUP_LANES = HEADS_PER_GROUP * B_HEAD_DIM
DILATED_PATTERNS = ((128, 1), (512, 4), (2048, 16))
WINDOW_KEYS = 128
ROPE_THETA = 10000.0
CONV_WIDTH = 31
CONV_HALO = 32
FFN_DIM = 2816
FFN_CONV_WIDTH = 3
FFN_HALO = 8
FFN_CHUNK = 256
EPS = 1e-6
NEG = -1e30

TOKEN_TILE = 512
V7X_VMEM_LIMIT_BYTES = 56 * 1024 * 1024


def _cparams(n_axes):
    return pltpu.CompilerParams(
        dimension_semantics=("arbitrary",) * n_axes,
        vmem_limit_bytes=V7X_VMEM_LIMIT_BYTES)


def _const_spec(shape):
    nd = len(shape)
    return pl.BlockSpec(shape, lambda *_: (0,) * nd, pipeline_mode=pl.Buffered(1))


def _dot(a, b):
    return jnp.dot(a, b, preferred_element_type=F32)


def _split_dot(a, b):
    hi = a.astype(BF16)
    lo = (a - hi.astype(F32)).astype(BF16)
    return _dot(hi, b) + _dot(lo, b)


def _sigmoid(x):
    return jax.nn.sigmoid(x)


def _gelu(x):
    return 0.5 * x * (1.0 + lax.erf(x * (1.0 / math.sqrt(2.0))))


def _modulate(x, g, shift, scale):
    ms = jnp.mean(x * x, axis=-1, keepdims=True)
    return (x * lax.rsqrt(ms + EPS) * g) * (1.0 + scale) + shift


def _ada_kernel(c_ref, w_ref, b_ref, o_ref):
    c = c_ref[...]
    act = (c * _sigmoid(c)).astype(BF16)
    o_ref[0] = _dot(act, w_ref[0].astype(BF16)) + b_ref[0]


def _ada(c, ada_w, ada_b):
    depth, d, n = ada_w.shape
    bn = c.shape[0]
    nb = 1536
    return pl.pallas_call(
        _ada_kernel,
        grid=(depth, n // nb),
        in_specs=[
            pl.BlockSpec((bn, d), lambda l, j: (0, 0)),
            pl.BlockSpec((1, d, nb), lambda l, j: (l, 0, j)),
            pl.BlockSpec((1, 1, nb), lambda l, j: (l, 0, j)),
        ],
        out_specs=pl.BlockSpec((1, bn, nb), lambda l, j: (l, 0, j)),
        out_shape=jax.ShapeDtypeStruct((depth, bn, n), F32),
        compiler_params=_cparams(2),
        name="ada",
    )(c, ada_w, ada_b.reshape(depth, 1, n))


def _rope_kernel(pos_ref, inv_ref, cos_ref, sin_ref):
    ang = pos_ref[...].astype(F32) * inv_ref[...]
    cos_ref[...] = jnp.cos(ang)
    sin_ref[...] = jnp.sin(ang)


def _rope_tables(positions):
    bn, s = positions.shape
    half = B_HEAD_DIM // 2
    inv_freq = 1.0 / (ROPE_THETA ** (jnp.arange(0, B_HEAD_DIM, 2, dtype=F32) / B_HEAD_DIM))
    per_row = 128 // half
    rows = bn * s // per_row
    pos_rep = jnp.repeat(positions.reshape(rows, per_row), half, axis=1)
    inv = jnp.tile(inv_freq, per_row).reshape(1, 128)
    rb = min(rows, 1024)
    cos, sin = pl.pallas_call(
        _rope_kernel,
        grid=(rows // rb,),
        in_specs=[pl.BlockSpec((rb, 128), lambda i: (i, 0)),
                  pl.BlockSpec((1, 128), lambda i: (0, 0))],
        out_specs=[pl.BlockSpec((rb, 128), lambda i: (i, 0))] * 2,
        out_shape=[jax.ShapeDtypeStruct((rows, 128), F32)] * 2,
        compiler_params=_cparams(1),
        name="rope",
    )(pos_rep, inv)
    cos = jnp.tile(cos.reshape(bn, s, half), (1, 1, HEADS_PER_GROUP))
    sin = jnp.tile(sin.reshape(bn, s, half), (1, 1, HEADS_PER_GROUP))
    return cos, sin


def _qk_perm():
    half = B_HEAD_DIM // 2
    perm = np.zeros(B_WIDTH, np.int32)
    for grp in range(B_HEADS // HEADS_PER_GROUP):
        for n in range(GROUP_LANES):
            head = grp * HEADS_PER_GROUP + (n % 128) // half
            perm[grp * GROUP_LANES + n] = head * B_HEAD_DIM + (n // 128) * half + n % half
    return perm


def _inproj_kernel(x_ref, mod_ref, g_ref, w_ref, vg_ref, ws_ref, wsb_ref, qg_ref, kg_ref,
                   cos_ref, sin_ref, ya_ref, q_ref, k_ref, v_ref):
    tm = x_ref.shape[1]
    x = x_ref[0]
    h = _modulate(x, g_ref[...], mod_ref[0, 0:1, :], mod_ref[0, 1:2, :]).astype(BF16)

    ua = _gelu(_dot(h, w_ref[:, 0:A_WIDTH]))
    va = _gelu(_dot(h, w_ref[:, A_WIDTH:2 * A_WIDTH]))
    row = lax.broadcasted_iota(jnp.int32, (CHUNK, CHUNK), 0)
    col = lax.broadcasted_iota(jnp.int32, (CHUNK, CHUNK), 1)
    causal = row >= col
    for g in range(A_GROUPS):
        lanes = slice(g * A_GROUP_DIM, (g + 1) * A_GROUP_DIM)
        vg = va[:, lanes]
        mu = jnp.mean(vg, axis=-1, keepdims=True)
        var = jnp.mean(jnp.square(vg - mu), axis=-1, keepdims=True)
        vn = ((vg - mu) * lax.rsqrt(var + EPS) * vg_ref[:, lanes]).astype(BF16)
        ws = jnp.where(causal, ws_ref[g], 0.0).astype(BF16)
        bias = wsb_ref[:, g:g + 1]
        for c in range(tm // CHUNK):
            rows = slice(c * CHUNK, (c + 1) * CHUNK)
            f = _dot(ws, vn[rows, :]) + bias
            ya_ref[0, rows, lanes] = (ua[rows, lanes] * f).astype(BF16)

    seg_r = lax.broadcasted_iota(jnp.int32, (128, 128), 0) // (B_HEAD_DIM // 2)
    seg_c = lax.broadcasted_iota(jnp.int32, (128, 128), 1) // (B_HEAD_DIM // 2)
    seg_ones = jnp.where(seg_r == seg_c, 1.0, 0.0).astype(BF16)
    cos = cos_ref[0]
    sin = sin_ref[0]

    def norm_rot(base, gain_ref, out_ref, out_scale):
        z = _dot(h, w_ref[:, base:base + B_WIDTH])
        for grp in range(B_HEADS // HEADS_PER_GROUP):
            lo = grp * GROUP_LANES
            z1 = z[:, lo:lo + 128]
            z2 = z[:, lo + 128:lo + 256]
            ss = _split_dot(z1 * z1 + z2 * z2, seg_ones)
            inv = lax.rsqrt(ss * (1.0 / B_HEAD_DIM) + EPS)
            a1 = z1 * inv * gain_ref[:, lo:lo + 128]
            a2 = z2 * inv * gain_ref[:, lo + 128:lo + 256]
            out_ref[0, :, lo:lo + 128] = ((a1 * cos - a2 * sin) * out_scale).astype(BF16)
            out_ref[0, :, lo + 128:lo + 256] = ((a2 * cos + a1 * sin) * out_scale).astype(BF16)

    norm_rot(2 * A_WIDTH, qg_ref, q_ref, B_HEAD_DIM ** -0.5)
    norm_rot(2 * A_WIDTH + B_WIDTH, kg_ref, k_ref, 1.0)
    v_ref[0] = _dot(h, w_ref[:, 2 * A_WIDTH + 2 * B_WIDTH:]).astype(BF16)


def _inproj(x, mod, norm_g, w_in, a_vnorm_g, a_spatial_w, a_spatial_b, q_g, k_g, cos, sin):
    bn, s, d = x.shape
    tm = TOKEN_TILE
    perm = _qk_perm()
    qs = 2 * A_WIDTH
    w = jnp.concatenate([w_in[:, :qs], w_in[:, qs:qs + B_WIDTH][:, perm],
                         w_in[:, qs + B_WIDTH:qs + 2 * B_WIDTH][:, perm],
                         w_in[:, qs + 2 * B_WIDTH:]], axis=1).astype(BF16)
    e_of = perm % B_HEAD_DIM
    tok = lambda width: pl.BlockSpec((1, tm, width), lambda b, i: (b, i, 0))
    out = jax.ShapeDtypeStruct((bn, s, B_WIDTH), BF16)
    return pl.pallas_call(
        _inproj_kernel,
        grid=(bn, s // tm),
        in_specs=[
            tok(d),
            pl.BlockSpec((1, 6, d), lambda b, i: (b, 0, 0)),
            _const_spec((1, d)),
            _const_spec(w.shape),
            _const_spec((1, A_WIDTH)),
            _const_spec((A_GROUPS, CHUNK, CHUNK)),
            _const_spec((CHUNK, A_GROUPS)),
            _const_spec((1, B_WIDTH)),
            _const_spec((1, B_WIDTH)),
            tok(128),
            tok(128),
        ],
        out_specs=[tok(A_WIDTH), tok(B_WIDTH), tok(B_WIDTH), tok(B_WIDTH)],
        out_shape=[jax.ShapeDtypeStruct((bn, s, A_WIDTH), BF16), out, out, out],
        compiler_params=_cparams(2),
        name="inproj",
    )(x, mod, norm_g.reshape(1, d), w, a_vnorm_g.reshape(1, A_WIDTH), a_spatial_w,
      a_spatial_b.T, q_g[e_of].reshape(1, B_WIDTH), k_g[e_of].reshape(1, B_WIDTH), cos, sin)


def _attn_kernel(qmask_ref, q_ref, kp_ref, kc_ref, vp_ref, vc_ref, o_ref, lse_ref):
    tq = q_ref.shape[1]
    n = pl.program_id(2)
    qi = lax.broadcasted_iota(jnp.int32, (WINDOW_KEYS, 2 * WINDOW_KEYS), 0)
    kj = lax.broadcasted_iota(jnp.int32, (WINDOW_KEYS, 2 * WINDOW_KEYS), 1)
    dist = qi + WINDOW_KEYS - kj
    band = (dist >= 0) & (dist <= WINDOW_KEYS)
    vlane = lax.broadcasted_iota(jnp.int32, (WINDOW_KEYS, GROUP_LANES), 1) // B_HEAD_DIM
    hlane = lax.broadcasted_iota(jnp.int32, (WINDOW_KEYS, 128), 1)
    for i in range(tq // WINDOW_KEYS):
        rows = slice(i * WINDOW_KEYS, (i + 1) * WINDOW_KEYS)
        if i == 0:
            k_prev, v_prev = kp_ref[0], vp_ref[0]
            mask = band & ((kj >= WINDOW_KEYS) | (n > 0))
        else:
            prev = slice((i - 1) * WINDOW_KEYS, i * WINDOW_KEYS)
            k_prev, v_prev = kc_ref[0, prev, :], vc_ref[0, prev, :]
            mask = band
        kk = jnp.concatenate([k_prev, kc_ref[0, rows, :]], axis=0)
        vv = jnp.concatenate([v_prev, vc_ref[0, rows, :]], axis=0)
        q = q_ref[0, rows, :]
        lse_tile = jnp.zeros((WINDOW_KEYS, 128), F32)
        for grp in range(B_HEADS // HEADS_PER_GROUP):
            lanes = slice(grp * GROUP_LANES, (grp + 1) * GROUP_LANES)
            qg, kg, vg = q[:, lanes], kk[:, lanes], vv[:, lanes]
            out = jnp.zeros((WINDOW_KEYS, GROUP_LANES), F32)
            for hh in range(HEADS_PER_GROUP):
                qm = jnp.where(qmask_ref[hh] > 0, qg, jnp.zeros_like(qg))
                sc = lax.dot_general(qm, kg, (((1,), (1,)), ((), ())), preferred_element_type=F32)
                sc = jnp.where(mask, sc, NEG)
                m = jnp.max(sc, axis=-1, keepdims=True)
                p = jnp.exp(sc - m)
                den = jnp.sum(p, axis=-1, keepdims=True)
                pv = _dot(p.astype(BF16), vg)
                out = jnp.where(vlane == hh, pv / den, out)
                lse_tile = jnp.where(hlane == grp * HEADS_PER_GROUP + hh, m + jnp.log(den), lse_tile)
            o_ref[0, rows, lanes] = out.astype(BF16)
        lse_ref[0, rows, :] = lse_tile


def _attn_branch(q, k, v, dilation, tq):
    bn, s, w = q.shape
    sub_len = s // dilation
    view = lambda t: t.reshape(bn, sub_len, dilation * w)
    blocks_per_step = tq // WINDOW_KEYS
    cur = pl.BlockSpec((1, tq, w), lambda b, r, n: (b, n, r))
    prev = pl.BlockSpec((1, WINDOW_KEYS, w),
                        lambda b, r, n: (b, jnp.maximum(n * blocks_per_step - 1, 0), r))
    half = B_HEAD_DIM // 2
    lane = np.arange(GROUP_LANES)
    qmask = np.stack([np.broadcast_to(((lane % 128) // half == hh), (WINDOW_KEYS, GROUP_LANES))
                      for hh in range(HEADS_PER_GROUP)]).astype(np.float32)
    o, lse = pl.pallas_call(
        _attn_kernel,
        grid=(bn, dilation, sub_len // tq),
        in_specs=[_const_spec(qmask.shape), cur, prev, cur, prev, cur],
        out_specs=[cur, pl.BlockSpec((1, tq, 128), lambda b, r, n: (b, n, r))],
        out_shape=[jax.ShapeDtypeStruct((bn, sub_len, dilation * w), BF16),
                   jax.ShapeDtypeStruct((bn, sub_len, dilation * 128), F32)],
        compiler_params=_cparams(3),
        name=f"attn_d{dilation}",
    )(jnp.asarray(qmask, BF16), view(q), view(k), view(k), view(v), view(v))
    return o.reshape(bn, s, w), lse.reshape(bn, s, 128)


def _ffn(x1, g, shift, scale, gate, up_ref, dww_ref, dwb_ref, down_ref, zs_ref, carry_ref):
    tm = x1.shape[0]
    h = _modulate(x1, g, shift, scale).astype(BF16)
    acc = None
    for c in range(FFN_DIM // FFN_CHUNK):
        cols = slice(c * 2 * FFN_CHUNK, (c + 1) * 2 * FFN_CHUNK)
        zs = zs_ref.at[c % 2]
        z = _dot(h, up_ref[:, cols])
        zs[0:FFN_HALO, :] = carry_ref[:, cols]
        zs[FFN_HALO:FFN_HALO + tm, :] = z
        carry_ref[:, cols] = z[tm - FFN_HALO:, :]
        y = (z * dww_ref[2:3, cols] + zs[FFN_HALO - 1:FFN_HALO - 1 + tm, :] * dww_ref[1:2, cols]
             + zs[FFN_HALO - 2:FFN_HALO - 2 + tm, :] * dww_ref[0:1, cols] + dwb_ref[:, cols])
        a = y[:, :FFN_CHUNK]
        act = (a * _sigmoid(a) * y[:, FFN_CHUNK:]).astype(BF16)
        d = _dot(act, down_ref[c * FFN_CHUNK:(c + 1) * FFN_CHUNK, :])
        acc = d if acc is None else acc + d
    return x1 + gate * acc


def _ffn_weights(up_w, dw_w, dw_b, down_w):
    nch = FFN_DIM // FFN_CHUNK

    def regroup(t):
        lead = t.shape[:-1]
        t = t.reshape(lead + (2, nch, FFN_CHUNK))
        return jnp.swapaxes(t, -3, -2).reshape(lead + (2 * FFN_DIM,))

    return (regroup(up_w).astype(BF16), regroup(dw_w), regroup(dw_b).reshape(1, 2 * FFN_DIM),
            down_w.astype(BF16))


def _ffn_specs():
    return [_const_spec((D_MODEL, 2 * FFN_DIM)), _const_spec((FFN_CONV_WIDTH, 2 * FFN_DIM)),
            _const_spec((1, 2 * FFN_DIM)), _const_spec((FFN_DIM, D_MODEL))]


def _ffn_scratch(tm):
    return [pltpu.VMEM((2, tm + FFN_HALO, 2 * FFN_CHUNK), F32), pltpu.VMEM((FFN_HALO, 2 * FFN_DIM), F32)]


def _mixffn_kernel(x_ref, mod_ref, ya_ref, o1_ref, o2_ref, o3_ref, l1_ref, l2_ref, l3_ref, exp_ref,
                   wout_ref, g_ref, up_ref, dww_ref, dwb_ref, down_ref, out_ref, zs_ref, carry_ref):
    @pl.when(pl.program_id(1) == 0)
    def _():
        carry_ref[...] = jnp.zeros_like(carry_ref)

    l1, l2, l3 = l1_ref[0], l2_ref[0], l3_ref[0]
    mx = jnp.maximum(l1, jnp.maximum(l2, l3))
    e1, e2, e3 = jnp.exp(l1 - mx), jnp.exp(l2 - mx), jnp.exp(l3 - mx)
    tot = e1 + e2 + e3
    expand = exp_ref[...]
    yb = (_split_dot(e1 / tot, expand) * o1_ref[0].astype(F32)
          + _split_dot(e2 / tot, expand) * o2_ref[0].astype(F32)
          + _split_dot(e3 / tot, expand) * o3_ref[0].astype(F32))
    y = _dot(ya_ref[0], wout_ref[0:A_WIDTH, :]) + _dot(yb.astype(BF16), wout_ref[A_WIDTH:, :])
    x1 = x_ref[0] + mod_ref[0, 2:3, :] * y
    out_ref[0] = _ffn(x1, g_ref[...], mod_ref[0, 3:4, :], mod_ref[0, 4:5, :], mod_ref[0, 5:6, :],
                      up_ref, dww_ref, dwb_ref, down_ref, zs_ref, carry_ref)


def _mixffn(x, mod, ya, outs, lses, w_out, norm_g, ffn_w):
    bn, s, d = x.shape
    tm = TOKEN_TILE
    tok = lambda width: pl.BlockSpec((1, tm, width), lambda b, i: (b, i, 0))
    expand = np.zeros((128, B_WIDTH), np.float32)
    for hd in range(B_HEADS):
        expand[hd, hd * B_HEAD_DIM:(hd + 1) * B_HEAD_DIM] = 1.0
    return pl.pallas_call(
        _mixffn_kernel,
        grid=(bn, s // tm),
        in_specs=[tok(d), pl.BlockSpec((1, 6, d), lambda b, i: (b, 0, 0)), tok(A_WIDTH),
                  tok(B_WIDTH), tok(B_WIDTH), tok(B_WIDTH), tok(128), tok(128), tok(128),
                  _const_spec((128, B_WIDTH)), _const_spec((A_WIDTH + B_WIDTH, d)),
                  _const_spec((1, d))] + _ffn_specs(),
        out_specs=tok(d),
        out_shape=jax.ShapeDtypeStruct((bn, s, d), F32),
        scratch_shapes=_ffn_scratch(tm),
        compiler_params=_cparams(2),
        name="mixffn",
    )(x, mod, ya, *outs, *lses, jnp.asarray(expand, BF16), w_out.astype(BF16),
      norm_g.reshape(1, d), *ffn_w)


def _convffn_kernel(x_ref, mod_ref, gm_ref, pw1_ref, pw1b_ref, dw_ref, dwb_ref, lng_ref, lnb_ref,
                    pw2_ref, pw2b_ref, g_ref, up_ref, fdww_ref, fdwb_ref, down_ref, out_ref,
                    ybuf_ref, cv_ref, zs_ref, carry_ref):
    tm = x_ref.shape[1]
    d = x_ref.shape[2]

    @pl.when(pl.program_id(1) == 0)
    def _():
        carry_ref[...] = jnp.zeros_like(carry_ref)
        ybuf_ref[0:CONV_HALO, :] = jnp.zeros((CONV_HALO, d), F32)

    x = x_ref[0]
    h = _modulate(x, gm_ref[...], mod_ref[0, 0:1, :], mod_ref[0, 1:2, :]).astype(BF16)
    a = _dot(h, pw1_ref[:, 0:d]) + pw1b_ref[:, 0:d]
    gate = _dot(h, pw1_ref[:, d:]) + pw1b_ref[:, d:]
    ybuf_ref[CONV_HALO:CONV_HALO + tm, :] = a * _sigmoid(gate)

    rb, lb = 128, 256
    first = CONV_HALO - (CONV_WIDTH - 1)
    for r in range(tm // rb):
        for l in range(d // lb):
            lanes = slice(l * lb, (l + 1) * lb)
            acc = jnp.zeros((rb, lb), F32) + dwb_ref[:, lanes]
            for k in range(CONV_WIDTH):
                start = r * rb + first + k
                acc = acc + ybuf_ref[start:start + rb, lanes] * dw_ref[k:k + 1, lanes]
            cv_ref[r * rb:(r + 1) * rb, lanes] = acc
    ybuf_ref[0:CONV_HALO, :] = ybuf_ref[tm:tm + CONV_HALO, :]

    cv = cv_ref[...]
    mu = jnp.mean(cv, axis=-1, keepdims=True)
    var = jnp.mean(jnp.square(cv - mu), axis=-1, keepdims=True)
    yn = (cv - mu) * lax.rsqrt(var + EPS) * lng_ref[...] + lnb_ref[...]
    act = (yn * _sigmoid(yn)).astype(BF16)
    y = _dot(act, pw2_ref[...]) + pw2b_ref[...]
    x1 = x + mod_ref[0, 2:3, :] * y
    out_ref[0] = _ffn(x1, g_ref[...], mod_ref[0, 3:4, :], mod_ref[0, 4:5, :], mod_ref[0, 5:6, :],
                      up_ref, fdww_ref, fdwb_ref, down_ref, zs_ref, carry_ref)


def _convffn(x, mod, norm_mix_g, pw1_w, pw1_b, dw_w, dw_b, ln_g, ln_b, pw2_w, pw2_b, norm_ffn_g, ffn_w):
    bn, s, d = x.shape
    tm = TOKEN_TILE
    tok = pl.BlockSpec((1, tm, d), lambda b, i: (b, i, 0))
    row = lambda t: t.reshape(1, -1)
    return pl.pallas_call(
        _convffn_kernel,
        grid=(bn, s // tm),
        in_specs=[tok, pl.BlockSpec((1, 6, d), lambda b, i: (b, 0, 0)), _const_spec((1, d)),
                  _const_spec((d, 2 * d)), _const_spec((1, 2 * d)), _const_spec((CONV_WIDTH, d)),
                  _const_spec((1, d)), _const_spec((1, d)), _const_spec((1, d)),
                  _const_spec((d, d)), _const_spec((1, d)), _const_spec((1, d))] + _ffn_specs(),
        out_specs=tok,
        out_shape=jax.ShapeDtypeStruct((bn, s, d), F32),
        scratch_shapes=[pltpu.VMEM((tm + CONV_HALO, d), F32), pltpu.VMEM((tm, d), F32)] + _ffn_scratch(tm),
        compiler_params=_cparams(2),
        name="convffn",
    )(x, mod, row(norm_mix_g), pw1_w.astype(BF16), row(pw1_b), dw_w, row(dw_b), row(ln_g), row(ln_b),
      pw2_w.astype(BF16), row(pw2_b), row(norm_ffn_g), *ffn_w)


def kernel(x, c, positions, ada_w, ada_b, norm_mix_g, norm_ffn_g, ab_w_in, a_vnorm_g, a_spatial_w, a_spatial_b, b_q_norm_g, b_k_norm_g, ab_w_out, conv_pw1_w, conv_pw1_b, conv_dw_w, conv_dw_b, conv_ln_g, conv_ln_b, conv_pw2_w, conv_pw2_b, ffn_up_w, ffn_dw_w, ffn_dw_b, ffn_down_w):
    bn, s, d = x.shape
    mod = _ada(c, ada_w, ada_b).reshape(ada_w.shape[0], bn, 6, d)
    cos, sin = _rope_tables(positions)

    ya, q, k, v = _inproj(x, mod[0], norm_mix_g[0], ab_w_in[0], a_vnorm_g[0], a_spatial_w[0],
                          a_spatial_b[0], b_q_norm_g[0], b_k_norm_g[0], cos, sin)
    outs, lses = [], []
    for window, dilation in DILATED_PATTERNS:
        assert window // dilation == WINDOW_KEYS
        tq = min(512, s // dilation)
        o, lse = _attn_branch(q, k, v, dilation, tq)
        outs.append(o)
        lses.append(lse)
    ffn0 = _ffn_weights(ffn_up_w[0], ffn_dw_w[0], ffn_dw_b[0], ffn_down_w[0])
    x = _mixffn(x, mod[0], ya, outs, lses, ab_w_out[0], norm_ffn_g[0], ffn0)

    ffn1 = _ffn_weights(ffn_up_w[1], ffn_dw_w[1], ffn_dw_b[1], ffn_down_w[1])
    return _convffn(x, mod[1], norm_mix_g[1], conv_pw1_w[0], conv_pw1_b[0], conv_dw_w[0], conv_dw_b[0],
                    conv_ln_g[0], conv_ln_b[0], conv_pw2_w[0], conv_pw2_b[0], norm_ffn_g[1], ffn1)
```

```python
import math

import numpy as np
import jax
import jax.numpy as jnp
from jax import lax
from jax.experimental import pallas as pl
from jax.experimental.pallas import tpu as pltpu

F32 = jnp.float32
BF16 = jnp.bfloat16

D_MODEL = 1024
A_WIDTH = 512
A_GROUPS = 4
A_GROUP_DIM = 128
CHUNK = 128
B_WIDTH = 512
B_HEAD_DIM = 64
B_HEADS = 8
HEADS_PER_GROUP = 4
GROUP_LANES = HEADS_PER_GROUP * B_HEAD_DIM
N_GROUPS = B_HEADS // HEADS_PER_GROUP
DILATED_PATTERNS = ((128, 1), (512, 4), (2048, 16))
WINDOW_KEYS = 128
ROPE_THETA = 10000.0
CONV_WIDTH = 31
CONV_HALO = 32
FFN_DIM = 2816
FFN_CONV_WIDTH = 3
FFN_HALO = 8
FFN_CHUNK = 256
EPS = 1e-6
NEG = -1e30

LANES = 128
SLABS = B_WIDTH // LANES
TOKEN_TILE = 512
V7X_VMEM_LIMIT_BYTES = 56 * 1024 * 1024


def _cparams(n_axes):
    return pltpu.CompilerParams(
        dimension_semantics=("arbitrary",) * n_axes,
        vmem_limit_bytes=V7X_VMEM_LIMIT_BYTES)


def _const_spec(shape):
    nd = len(shape)
    return pl.BlockSpec(shape, lambda *_: (0,) * nd, pipeline_mode=pl.Buffered(1))


def _dot(a, b):
    return jnp.dot(a, b, preferred_element_type=F32)


def _split_dot(a, b):
    hi = a.astype(BF16)
    lo = (a - hi.astype(F32)).astype(BF16)
    return _dot(hi, b) + _dot(lo, b)


def _sigmoid(x):
    return jax.nn.sigmoid(x)


def _gelu(x):
    return 0.5 * x * (1.0 + lax.erf(x * (1.0 / math.sqrt(2.0))))


def _modulate(x, g, shift, scale):
    ms = jnp.mean(x * x, axis=-1, keepdims=True)
    return (x * lax.rsqrt(ms + EPS) * g) * (1.0 + scale) + shift


def _ada_kernel(c_ref, w_ref, b_ref, o_ref):
    c = c_ref[...]
    act = (c * _sigmoid(c)).astype(BF16)
    o_ref[0] = _dot(act, w_ref[0].astype(BF16)) + b_ref[0]


def _ada(c, ada_w, ada_b):
    depth, d, n = ada_w.shape
    bn = c.shape[0]
    nb = 1536
    return pl.pallas_call(
        _ada_kernel,
        grid=(depth, n // nb),
        in_specs=[
            pl.BlockSpec((bn, d), lambda l, j: (0, 0)),
            pl.BlockSpec((1, d, nb), lambda l, j: (l, 0, j)),
            pl.BlockSpec((1, 1, nb), lambda l, j: (l, 0, j)),
        ],
        out_specs=pl.BlockSpec((1, bn, nb), lambda l, j: (l, 0, j)),
        out_shape=jax.ShapeDtypeStruct((depth, bn, n), F32),
        compiler_params=_cparams(2),
        name="ada",
    )(c, ada_w, ada_b.reshape(depth, 1, n))


def _rope_kernel(pos_ref, inv_ref, cos_ref, sin_ref):
    ang = pos_ref[...].astype(F32) * inv_ref[...]
    cos_ref[...] = jnp.cos(ang)
    sin_ref[...] = jnp.sin(ang)


def _rope_tables(positions):
    bn, s = positions.shape
    half = B_HEAD_DIM // 2
    inv_freq = 1.0 / (ROPE_THETA ** (jnp.arange(0, B_HEAD_DIM, 2, dtype=F32) / B_HEAD_DIM))
    per_row = LANES // half
    rows = bn * s // per_row
    pos_rep = jnp.repeat(positions.reshape(rows, per_row), half, axis=1)
    inv = jnp.tile(inv_freq, per_row).reshape(1, LANES)
    rb = min(rows, 1024)
    cos, sin = pl.pallas_call(
        _rope_kernel,
        grid=(rows // rb,),
        in_specs=[pl.BlockSpec((rb, LANES), lambda i: (i, 0)),
                  pl.BlockSpec((1, LANES), lambda i: (0, 0))],
        out_specs=[pl.BlockSpec((rb, LANES), lambda i: (i, 0))] * 2,
        out_shape=[jax.ShapeDtypeStruct((rows, LANES), F32)] * 2,
        compiler_params=_cparams(1),
        name="rope",
    )(pos_rep, inv)
    cos = jnp.tile(cos.reshape(bn, s, half), (1, 1, HEADS_PER_GROUP))
    sin = jnp.tile(sin.reshape(bn, s, half), (1, 1, HEADS_PER_GROUP))
    return cos, sin


def _qk_perm():
    half = B_HEAD_DIM // 2
    perm = np.zeros(B_WIDTH, np.int32)
    for grp in range(N_GROUPS):
        for n in range(GROUP_LANES):
            head = grp * HEADS_PER_GROUP + (n % LANES) // half
            perm[grp * GROUP_LANES + n] = head * B_HEAD_DIM + (n // LANES) * half + n % half
    return perm


def _deinterleave(tok_ref, r4_ref, out1_ref, out4_ref, out16_ref):
    tm = tok_ref.shape[1]
    q4, q16 = tm // 4, tm // 16
    for c in range(SLABS):
        lanes = slice(c * LANES, (c + 1) * LANES)
        out1_ref[0, :, lanes] = tok_ref[c].astype(BF16)
        for r in range(4):
            blk = tok_ref[c, pl.ds(r, q4, stride=4), :]
            out4_ref[0, r, :, lanes] = blk.astype(BF16)
            r4_ref[c, r * q4:(r + 1) * q4, :] = blk
        for r in range(4):
            for r2 in range(4):
                blk = r4_ref[c, pl.ds(r * q4 + r2, q16, stride=4), :]
                out16_ref[0, 4 * r2 + r, :, lanes] = blk.astype(BF16)


def _inproj_kernel(x_ref, mod_ref, g_ref, w_ref, vg_ref, ws_ref, wsb_ref, qg_ref, kg_ref,
                   cos_ref, sin_ref, ya_ref, q1_ref, k1_ref, v1_ref, q4_ref, k4_ref, v4_ref,
                   q16_ref, k16_ref, v16_ref, tok_ref, r4_ref):
    tm = x_ref.shape[1]
    x = x_ref[0]
    h = _modulate(x, g_ref[...], mod_ref[0, 0:1, :], mod_ref[0, 1:2, :]).astype(BF16)

    ua = _gelu(_dot(h, w_ref[:, 0:A_WIDTH]))
    va = _gelu(_dot(h, w_ref[:, A_WIDTH:2 * A_WIDTH]))
    row = lax.broadcasted_iota(jnp.int32, (CHUNK, CHUNK), 0)
    col = lax.broadcasted_iota(jnp.int32, (CHUNK, CHUNK), 1)
    causal = row >= col
    for g in range(A_GROUPS):
        lanes = slice(g * A_GROUP_DIM, (g + 1) * A_GROUP_DIM)
        vg = va[:, lanes]
        mu = jnp.mean(vg, axis=-1, keepdims=True)
        var = jnp.mean(jnp.square(vg - mu), axis=-1, keepdims=True)
        vn = ((vg - mu) * lax.rsqrt(var + EPS) * vg_ref[:, lanes]).astype(BF16)
        ws = jnp.where(causal, ws_ref[g], 0.0).astype(BF16)
        bias = wsb_ref[:, g:g + 1]
        for c in range(tm // CHUNK):
            rows = slice(c * CHUNK, (c + 1) * CHUNK)
            f = _dot(ws, vn[rows, :]) + bias
            ya_ref[0, rows, lanes] = (ua[rows, lanes] * f).astype(BF16)

    seg_r = lax.broadcasted_iota(jnp.int32, (LANES, LANES), 0) // (B_HEAD_DIM // 2)
    seg_c = lax.broadcasted_iota(jnp.int32, (LANES, LANES), 1) // (B_HEAD_DIM // 2)
    seg_ones = jnp.where(seg_r == seg_c, 1.0, 0.0).astype(BF16)
    cos = cos_ref[0]
    sin = sin_ref[0]

    def norm_rot(base, gain_ref, tok, out_scale):
        z = _dot(h, w_ref[:, base:base + B_WIDTH])
        for grp in range(N_GROUPS):
            lo = grp * GROUP_LANES
            z1 = z[:, lo:lo + LANES]
            z2 = z[:, lo + LANES:lo + 2 * LANES]
            ss = _split_dot(z1 * z1 + z2 * z2, seg_ones)
            inv = lax.rsqrt(ss * (1.0 / B_HEAD_DIM) + EPS)
            a1 = z1 * inv * gain_ref[:, lo:lo + LANES]
            a2 = z2 * inv * gain_ref[:, lo + LANES:lo + 2 * LANES]
            tok[2 * grp] = (a1 * cos - a2 * sin) * out_scale
            tok[2 * grp + 1] = (a2 * cos + a1 * sin) * out_scale

    norm_rot(2 * A_WIDTH, qg_ref, tok_ref.at[0], B_HEAD_DIM ** -0.5)
    _deinterleave(tok_ref.at[0], r4_ref.at[0], q1_ref, q4_ref, q16_ref)
    norm_rot(2 * A_WIDTH + B_WIDTH, kg_ref, tok_ref.at[1], 1.0)
    _deinterleave(tok_ref.at[1], r4_ref.at[1], k1_ref, k4_ref, k16_ref)
    v = _dot(h, w_ref[:, 2 * A_WIDTH + 2 * B_WIDTH:])
    for c in range(SLABS):
        tok_ref[2, c] = v[:, c * LANES:(c + 1) * LANES]
    _deinterleave(tok_ref.at[2], r4_ref.at[2], v1_ref, v4_ref, v16_ref)


def _inproj(x, mod, norm_g, w_in, a_vnorm_g, a_spatial_w, a_spatial_b, q_g, k_g, cos, sin):
    bn, s, d = x.shape
    tm = TOKEN_TILE
    perm = _qk_perm()
    qs = 2 * A_WIDTH
    w = jnp.concatenate([w_in[:, :qs], w_in[:, qs:qs + B_WIDTH][:, perm],
                         w_in[:, qs + B_WIDTH:qs + 2 * B_WIDTH][:, perm],
                         w_in[:, qs + 2 * B_WIDTH:]], axis=1).astype(BF16)
    e_of = perm % B_HEAD_DIM
    tok = lambda width: pl.BlockSpec((1, tm, width), lambda b, i: (b, i, 0))
    res = lambda dil: pl.BlockSpec((1, dil, tm // dil, B_WIDTH), lambda b, i: (b, 0, i, 0))
    sds1 = jax.ShapeDtypeStruct((bn, s, B_WIDTH), BF16)
    sds = lambda dil: jax.ShapeDtypeStruct((bn, dil, s // dil, B_WIDTH), BF16)
    outs = pl.pallas_call(
        _inproj_kernel,
        grid=(bn, s // tm),
        in_specs=[
            tok(d),
            pl.BlockSpec((1, 6, d), lambda b, i: (b, 0, 0)),
            _const_spec((1, d)),
            _const_spec(w.shape),
            _const_spec((1, A_WIDTH)),
            _const_spec((A_GROUPS, CHUNK, CHUNK)),
            _const_spec((CHUNK, A_GROUPS)),
            _const_spec((1, B_WIDTH)),
            _const_spec((1, B_WIDTH)),
            tok(LANES),
            tok(LANES),
        ],
        out_specs=[tok(A_WIDTH)] + [tok(B_WIDTH)] * 3 + [res(4)] * 3 + [res(16)] * 3,
        out_shape=[jax.ShapeDtypeStruct((bn, s, A_WIDTH), BF16)] + [sds1] * 3 + [sds(4)] * 3 + [sds(16)] * 3,
        scratch_shapes=[pltpu.VMEM((3, SLABS, tm, LANES), F32), pltpu.VMEM((3, SLABS, tm, LANES), F32)],
        compiler_params=_cparams(2),
        name="inproj",
    )(x, mod, norm_g.reshape(1, d), w, a_vnorm_g.reshape(1, A_WIDTH), a_spatial_w,
      a_spatial_b.T, q_g[e_of].reshape(1, B_WIDTH), k_g[e_of].reshape(1, B_WIDTH), cos, sin)
    ya = outs[0]
    qkv = {1: [t.reshape(bn, 1, s, B_WIDTH) for t in outs[1:4]], 4: outs[4:7], 16: outs[7:10]}
    return ya, qkv


def _attn_kernel(qmask_ref, vmask_ref, q_ref, kp_ref, kc_ref, vp_ref, vc_ref, o_ref, lse_ref):
    tq = q_ref.shape[0]
    nsub = tq // WINDOW_KEYS
    n = pl.program_id(2)
    qi = lax.broadcasted_iota(jnp.int32, (WINDOW_KEYS, 2 * WINDOW_KEYS), 0)
    kj = lax.broadcasted_iota(jnp.int32, (WINDOW_KEYS, 2 * WINDOW_KEYS), 1)
    dist = qi + WINDOW_KEYS - kj
    band = (dist >= 0) & (dist <= WINDOW_KEYS)
    first_mask = band & ((kj >= WINDOW_KEYS) | (n > 0))
    vlane = lax.broadcasted_iota(jnp.int32, (WINDOW_KEYS, GROUP_LANES), 1) // B_HEAD_DIM

    def window(cur_ref, prev_ref, i):
        rows = slice(i * WINDOW_KEYS, (i + 1) * WINDOW_KEYS)
        prev = prev_ref[...] if i == 0 else cur_ref[(i - 1) * WINDOW_KEYS:i * WINDOW_KEYS, :]
        return jnp.concatenate([prev, cur_ref[rows, :]], axis=0)

    def scores(i):
        rows = slice(i * WINDOW_KEYS, (i + 1) * WINDOW_KEYS)
        kk = window(kc_ref, kp_ref, i)
        q = q_ref[rows, :]
        mask = first_mask if i == 0 else band
        out = []
        for grp in range(N_GROUPS):
            lanes = slice(grp * GROUP_LANES, (grp + 1) * GROUP_LANES)
            qg, kg = q[:, lanes], kk[:, lanes]
            for hh in range(HEADS_PER_GROUP):
                qm = jnp.where(qmask_ref[hh] > 0, qg, jnp.zeros_like(qg))
                sc = lax.dot_general(qm, kg, (((1,), (1,)), ((), ())), preferred_element_type=F32)
                out.append(jnp.where(mask, sc, NEG))
        return out

    def finish(i, scs):
        rows = slice(i * WINDOW_KEYS, (i + 1) * WINDOW_KEYS)
        vv = window(vc_ref, vp_ref, i)
        head_sel = vmask_ref[...]
        for grp in range(N_GROUPS):
            lanes = slice(grp * GROUP_LANES, (grp + 1) * GROUP_LANES)
            vg = vv[:, lanes]
            v_bd = jnp.concatenate([vg] * HEADS_PER_GROUP, axis=0)
            v_bd = jnp.where(head_sel > 0, v_bd, jnp.zeros_like(v_bd))
            ps, ms = [], []
            for hh in range(HEADS_PER_GROUP):
                sc = scs[grp * HEADS_PER_GROUP + hh]
                m = jnp.max(sc, axis=-1, keepdims=True)
                ps.append(jnp.exp(sc - m).astype(BF16))
                ms.append(m)
            p_cat = jnp.concatenate(ps, axis=1)
            pv = _dot(p_cat, v_bd)
            den = _dot(p_cat, head_sel)
            m_cat = jnp.broadcast_to(ms[-1], (WINDOW_KEYS, GROUP_LANES))
            for hh in range(HEADS_PER_GROUP - 1):
                m_cat = jnp.where(vlane == hh, ms[hh], m_cat)
            o_ref[rows, lanes] = (pv / den).astype(BF16)
            lse_ref[rows, lanes] = m_cat + jnp.log(den)

    pending = scores(0)
    for i in range(nsub):
        nxt = scores(i + 1) if i + 1 < nsub else None
        finish(i, pending)
        pending = nxt


def _attn_branch(q, k, v, tq):
    bn, dilation, sub_len, w = q.shape
    blocks_per_step = tq // WINDOW_KEYS
    cur = pl.BlockSpec((None, None, tq, w), lambda b, r, n: (b, r, n, 0))
    prev = pl.BlockSpec((None, None, WINDOW_KEYS, w),
                        lambda b, r, n: (b, r, jnp.maximum(n * blocks_per_step - 1, 0), 0))
    half = B_HEAD_DIM // 2
    lane = np.arange(GROUP_LANES)
    qmask = np.stack([np.broadcast_to(((lane % LANES) // half == hh), (WINDOW_KEYS, GROUP_LANES))
                      for hh in range(HEADS_PER_GROUP)]).astype(np.float32)
    vmask = np.concatenate([np.broadcast_to((lane // B_HEAD_DIM == hh), (2 * WINDOW_KEYS, GROUP_LANES))
                            for hh in range(HEADS_PER_GROUP)]).astype(np.float32)
    return pl.pallas_call(
        _attn_kernel,
        grid=(bn, dilation, sub_len // tq),
        in_specs=[_const_spec(qmask.shape), _const_spec(vmask.shape), cur, prev, cur, prev, cur],
        out_specs=[cur, cur],
        out_shape=[jax.ShapeDtypeStruct(q.shape, BF16), jax.ShapeDtypeStruct(q.shape, F32)],
        compiler_params=_cparams(3),
        name=f"attn_d{dilation}",
    )(jnp.asarray(qmask, BF16), jnp.asarray(vmask, BF16), q, k, k, v, v)


def _ffn(x1, g, shift, scale, gate, up_ref, dww_ref, dwb_ref, down_ref, zs_ref, carry_ref):
    tm = x1.shape[0]
    h = _modulate(x1, g, shift, scale).astype(BF16)
    acc = None
    for c in range(FFN_DIM // FFN_CHUNK):
        cols = slice(c * 2 * FFN_CHUNK, (c + 1) * 2 * FFN_CHUNK)
        zs = zs_ref.at[c % 2]
        z = _dot(h, up_ref[:, cols])
        zs[0:FFN_HALO, :] = carry_ref[:, cols]
        zs[FFN_HALO:FFN_HALO + tm, :] = z
        carry_ref[:, cols] = z[tm - FFN_HALO:, :]
        y = (z * dww_ref[2:3, cols] + zs[FFN_HALO - 1:FFN_HALO - 1 + tm, :] * dww_ref[1:2, cols]
             + zs[FFN_HALO - 2:FFN_HALO - 2 + tm, :] * dww_ref[0:1, cols] + dwb_ref[:, cols])
        a = y[:, :FFN_CHUNK]
        act = (a * _sigmoid(a) * y[:, FFN_CHUNK:]).astype(BF16)
        d = _dot(act, down_ref[c * FFN_CHUNK:(c + 1) * FFN_CHUNK, :])
        acc = d if acc is None else acc + d
    return x1 + gate * acc


def _ffn_weights(up_w, dw_w, dw_b, down_w):
    nch = FFN_DIM // FFN_CHUNK

    def regroup(t):
        lead = t.shape[:-1]
        t = t.reshape(lead + (2, nch, FFN_CHUNK))
        return jnp.swapaxes(t, -3, -2).reshape(lead + (2 * FFN_DIM,))

    return (regroup(up_w).astype(BF16), regroup(dw_w), regroup(dw_b).reshape(1, 2 * FFN_DIM),
            down_w.astype(BF16))


def _ffn_specs():
    return [_const_spec((D_MODEL, 2 * FFN_DIM)), _const_spec((FFN_CONV_WIDTH, 2 * FFN_DIM)),
            _const_spec((1, 2 * FFN_DIM)), _const_spec((FFN_DIM, D_MODEL))]


def _ffn_scratch(tm):
    return [pltpu.VMEM((2, tm + FFN_HALO, 2 * FFN_CHUNK), F32), pltpu.VMEM((FFN_HALO, 2 * FFN_DIM), F32)]


def _interleave4(src_ref, dst_ref):
    q4 = src_ref.shape[1]
    for c in range(SLABS):
        for r in range(4):
            dst_ref[c, pl.ds(r, q4, stride=4), :] = src_ref[r, :, c * LANES:(c + 1) * LANES].astype(F32)


def _interleave16(src_ref, mid_ref, dst_ref):
    q16 = src_ref.shape[1]
    q4 = 4 * q16
    for c in range(SLABS):
        for r in range(4):
            for r2 in range(4):
                mid_ref[c, pl.ds(r * q4 + r2, q16, stride=4), :] = (
                    src_ref[4 * r2 + r, :, c * LANES:(c + 1) * LANES].astype(F32))
        for r in range(4):
            dst_ref[c, pl.ds(r, q4, stride=4), :] = mid_ref[c, r * q4:(r + 1) * q4, :]


def _slabs(ref):
    return jnp.concatenate([ref[c] for c in range(SLABS)], axis=1)


def _mixffn_kernel(x_ref, mod_ref, ya_ref, o1_ref, l1_ref, o4_ref, l4_ref, o16_ref, l16_ref,
                   wout_ref, g_ref, up_ref, dww_ref, dwb_ref, down_ref, out_ref,
                   il_ref, mid_ref, zs_ref, carry_ref):
    @pl.when(pl.program_id(1) == 0)
    def _():
        carry_ref[...] = jnp.zeros_like(carry_ref)

    _interleave4(o4_ref, il_ref.at[0])
    _interleave4(l4_ref, il_ref.at[1])
    _interleave16(o16_ref, mid_ref.at[0], il_ref.at[2])
    _interleave16(l16_ref, mid_ref.at[1], il_ref.at[3])
    o1, l1 = o1_ref[...].astype(F32), l1_ref[...]
    o2, l2 = _slabs(il_ref.at[0]), _slabs(il_ref.at[1])
    o3, l3 = _slabs(il_ref.at[2]), _slabs(il_ref.at[3])
    mx = jnp.maximum(l1, jnp.maximum(l2, l3))
    e1, e2, e3 = jnp.exp(l1 - mx), jnp.exp(l2 - mx), jnp.exp(l3 - mx)
    yb = (e1 * o1 + e2 * o2 + e3 * o3) / (e1 + e2 + e3)
    y = _dot(ya_ref[0], wout_ref[0:A_WIDTH, :]) + _dot(yb.astype(BF16), wout_ref[A_WIDTH:, :])
    x1 = x_ref[0] + mod_ref[0, 2:3, :] * y
    out_ref[0] = _ffn(x1, g_ref[...], mod_ref[0, 3:4, :], mod_ref[0, 4:5, :], mod_ref[0, 5:6, :],
                      up_ref, dww_ref, dwb_ref, down_ref, zs_ref, carry_ref)


def _mixffn(x, mod, ya, branches, w_out, norm_g, ffn_w):
    bn, s, d = x.shape
    tm = TOKEN_TILE
    tok = lambda width: pl.BlockSpec((1, tm, width), lambda b, i: (b, i, 0))
    res = lambda dil: pl.BlockSpec((None, dil, tm // dil, B_WIDTH), lambda b, i: (b, 0, i, 0))
    branch_specs = [pl.BlockSpec((None, None, tm, B_WIDTH), lambda b, i: (b, 0, i, 0))] * 2
    branch_specs += [res(4)] * 2 + [res(16)] * 2
    return pl.pallas_call(
        _mixffn_kernel,
        grid=(bn, s // tm),
        in_specs=[tok(d), pl.BlockSpec((1, 6, d), lambda b, i: (b, 0, 0)), tok(A_WIDTH)] + branch_specs
                 + [_const_spec((A_WIDTH + B_WIDTH, d)), _const_spec((1, d))] + _ffn_specs(),
        out_specs=tok(d),
        out_shape=jax.ShapeDtypeStruct((bn, s, d), F32),
        scratch_shapes=[pltpu.VMEM((4, SLABS, tm, LANES), F32), pltpu.VMEM((2, SLABS, tm, LANES), F32)]
                       + _ffn_scratch(tm),
        compiler_params=_cparams(2),
        name="mixffn",
    )(x, mod, ya, *branches, w_out.astype(BF16), norm_g.reshape(1, d), *ffn_w)


def _convffn_kernel(x_ref, mod_ref, gm_ref, pw1_ref, pw1b_ref, dw_ref, dwb_ref, lng_ref, lnb_ref,
                    pw2_ref, pw2b_ref, g_ref, up_ref, fdww_ref, fdwb_ref, down_ref, out_ref,
                    ybuf_ref, cv_ref, zs_ref, carry_ref):
    tm = x_ref.shape[1]
    d = x_ref.shape[2]

    @pl.when(pl.program_id(1) == 0)
    def _():
        carry_ref[...] = jnp.zeros_like(carry_ref)
        ybuf_ref[0:CONV_HALO, :] = jnp.zeros((CONV_HALO, d), F32)

    x = x_ref[0]
    h = _modulate(x, gm_ref[...], mod_ref[0, 0:1, :], mod_ref[0, 1:2, :]).astype(BF16)
    a = _dot(h, pw1_ref[:, 0:d]) + pw1b_ref[:, 0:d]
    gate = _dot(h, pw1_ref[:, d:]) + pw1b_ref[:, d:]
    ybuf_ref[CONV_HALO:CONV_HALO + tm, :] = a * _sigmoid(gate)

    rb, lb = 128, 256
    first = CONV_HALO - (CONV_WIDTH - 1)
    for r in range(tm // rb):
        for l in range(d // lb):
            lanes = slice(l * lb, (l + 1) * lb)
            acc = jnp.zeros((rb, lb), F32) + dwb_ref[:, lanes]
            for k in range(CONV_WIDTH):
                start = r * rb + first + k
                acc = acc + ybuf_ref[start:start + rb, lanes] * dw_ref[k:k + 1, lanes]
            cv_ref[r * rb:(r + 1) * rb, lanes] = acc
    ybuf_ref[0:CONV_HALO, :] = ybuf_ref[tm:tm + CONV_HALO, :]

    cv = cv_ref[...]
    mu = jnp.mean(cv, axis=-1, keepdims=True)
    var = jnp.mean(jnp.square(cv - mu), axis=-1, keepdims=True)
    yn = (cv - mu) * lax.rsqrt(var + EPS) * lng_ref[...] + lnb_ref[...]
    act = (yn * _sigmoid(yn)).astype(BF16)
    y = _dot(act, pw2_ref[...]) + pw2b_ref[...]
    x1 = x + mod_ref[0, 2:3, :] * y
    out_ref[0] = _ffn(x1, g_ref[...], mod_ref[0, 3:4, :], mod_ref[0, 4:5, :], mod_ref[0, 5:6, :],
                      up_ref, fdww_ref, fdwb_ref, down_ref, zs_ref, carry_ref)


def _convffn(x, mod, norm_mix_g, pw1_w, pw1_b, dw_w, dw_b, ln_g, ln_b, pw2_w, pw2_b, norm_ffn_g, ffn_w):
    bn, s, d = x.shape
    tm = TOKEN_TILE
    tok = pl.BlockSpec((1, tm, d), lambda b, i: (b, i, 0))
    row = lambda t: t.reshape(1, -1)
    return pl.pallas_call(
        _convffn_kernel,
        grid=(bn, s // tm),
        in_specs=[tok, pl.BlockSpec((1, 6, d), lambda b, i: (b, 0, 0)), _const_spec((1, d)),
                  _const_spec((d, 2 * d)), _const_spec((1, 2 * d)), _const_spec((CONV_WIDTH, d)),
                  _const_spec((1, d)), _const_spec((1, d)), _const_spec((1, d)),
                  _const_spec((d, d)), _const_spec((1, d)), _const_spec((1, d))] + _ffn_specs(),
        out_specs=tok,
        out_shape=jax.ShapeDtypeStruct((bn, s, d), F32),
        scratch_shapes=[pltpu.VMEM((tm + CONV_HALO, d), F32), pltpu.VMEM((tm, d), F32)] + _ffn_scratch(tm),
        compiler_params=_cparams(2),
        name="convffn",
    )(x, mod, row(norm_mix_g), pw1_w.astype(BF16), row(pw1_b), dw_w, row(dw_b), row(ln_g), row(ln_b),
      pw2_w.astype(BF16), row(pw2_b), row(norm_ffn_g), *ffn_w)


def kernel(x, c, positions, ada_w, ada_b, norm_mix_g, norm_ffn_g, ab_w_in, a_vnorm_g, a_spatial_w, a_spatial_b, b_q_norm_g, b_k_norm_g, ab_w_out, conv_pw1_w, conv_pw1_b, conv_dw_w, conv_dw_b, conv_ln_g, conv_ln_b, conv_pw2_w, conv_pw2_b, ffn_up_w, ffn_dw_w, ffn_dw_b, ffn_down_w):
    bn, s, d = x.shape
    mod = _ada(c, ada_w, ada_b).reshape(ada_w.shape[0], bn, 6, d)
    cos, sin = _rope_tables(positions)

    ya, qkv = _inproj(x, mod[0], norm_mix_g[0], ab_w_in[0], a_vnorm_g[0], a_spatial_w[0],
                      a_spatial_b[0], b_q_norm_g[0], b_k_norm_g[0], cos, sin)
    branches = []
    for window, dilation in DILATED_PATTERNS:
        assert window // dilation == WINDOW_KEYS
        tq = min(512, s // dilation)
        branches += _attn_branch(*qkv[dilation], tq)
    ffn0 = _ffn_weights(ffn_up_w[0], ffn_dw_w[0], ffn_dw_b[0], ffn_down_w[0])
    x = _mixffn(x, mod[0], ya, branches, ab_w_out[0], norm_ffn_g[0], ffn0)

    ffn1 = _ffn_weights(ffn_up_w[1], ffn_dw_w[1], ffn_dw_b[1], ffn_down_w[1])
    return _convffn(x, mod[1], norm_mix_g[1], conv_pw1_w[0], conv_pw1_b[0], conv_dw_w[0], conv_dw_b[0],
                    conv_ln_g[0], conv_ln_b[0], conv_pw2_w[0], conv_pw2_b[0], norm_ffn_g[1], ffn1)
```

```python
import math

import numpy as np
import jax
import jax.numpy as jnp
from jax import lax
from jax.experimental import pallas as pl
from jax.experimental.pallas import tpu as pltpu

F32 = jnp.float32
BF16 = jnp.bfloat16

D_MODEL = 1024
A_WIDTH = 512
A_GROUPS = 4
A_GROUP_DIM = 128
CHUNK = 128
B_WIDTH = 512
B_HEAD_DIM = 64
B_HEADS = 8
HEADS_PER_GROUP = 4
GROUP_LANES = HEADS_PER_GROUP * B_HEAD_DIM
N_GROUPS = B_HEADS // HEADS_PER_GROUP
DILATED_PATTERNS = ((128, 1), (512, 4), (2048, 16))
WINDOW_KEYS = 128
ROPE_THETA = 10000.0
CONV_WIDTH = 31
CONV_ROW_BLOCK = 128
CONV_LANE_BLOCK = 256
FFN_DIM = 2816
FFN_CONV_WIDTH = 3
FFN_CHUNK = 256
TB_TOKENS = 64
TB_PITCH = 72
EPS = 1e-6
NEG = -1e30

LANES = 128
SLABS = B_WIDTH // LANES
TOKEN_TILE = 512
V7X_VMEM_LIMIT_BYTES = 56 * 1024 * 1024


def _cparams(n_axes):
    return pltpu.CompilerParams(
        dimension_semantics=("arbitrary",) * n_axes,
        vmem_limit_bytes=V7X_VMEM_LIMIT_BYTES)


def _const_spec(shape):
    nd = len(shape)
    return pl.BlockSpec(shape, lambda *_: (0,) * nd, pipeline_mode=pl.Buffered(1))


def _dot(a, b):
    return jnp.dot(a, b, preferred_element_type=F32)


def _split_dot(a, b):
    hi = a.astype(BF16)
    lo = (a - hi.astype(F32)).astype(BF16)
    return _dot(hi, b) + _dot(lo, b)


def _sigmoid(x):
    return jax.nn.sigmoid(x)


def _gelu(x):
    return 0.5 * x * (1.0 + lax.erf(x * (1.0 / math.sqrt(2.0))))


def _modulate(x, g, shift, scale):
    ms = jnp.mean(x * x, axis=-1, keepdims=True)
    return (x * lax.rsqrt(ms + EPS) * g) * (1.0 + scale) + shift


def _ada_kernel(c_ref, w_ref, b_ref, o_ref):
    c = c_ref[...]
    act = (c * _sigmoid(c)).astype(BF16)
    o_ref[0] = _dot(act, w_ref[0].astype(BF16)) + b_ref[0]


def _ada(c, ada_w, ada_b):
    depth, d, n = ada_w.shape
    bn = c.shape[0]
    nb = 1536
    return pl.pallas_call(
        _ada_kernel,
        grid=(depth, n // nb),
        in_specs=[
            pl.BlockSpec((bn, d), lambda l, j: (0, 0)),
            pl.BlockSpec((1, d, nb), lambda l, j: (l, 0, j)),
            pl.BlockSpec((1, 1, nb), lambda l, j: (l, 0, j)),
        ],
        out_specs=pl.BlockSpec((1, bn, nb), lambda l, j: (l, 0, j)),
        out_shape=jax.ShapeDtypeStruct((depth, bn, n), F32),
        compiler_params=_cparams(2),
        name="ada",
    )(c, ada_w, ada_b.reshape(depth, 1, n))


def _rope_kernel(pos_ref, inv_ref, cos_ref, sin_ref):
    ang = pos_ref[...].astype(F32) * inv_ref[...]
    cos_ref[...] = jnp.cos(ang)
    sin_ref[...] = jnp.sin(ang)


def _rope_tables(positions):
    bn, s = positions.shape
    half = B_HEAD_DIM // 2
    inv_freq = 1.0 / (ROPE_THETA ** (jnp.arange(0, B_HEAD_DIM, 2, dtype=F32) / B_HEAD_DIM))
    per_row = LANES // half
    rows = bn * s // per_row
    pos_rep = jnp.repeat(positions.reshape(rows, per_row), half, axis=1)
    inv = jnp.tile(inv_freq, per_row).reshape(1, LANES)
    rb = min(rows, 1024)
    cos, sin = pl.pallas_call(
        _rope_kernel,
        grid=(rows // rb,),
        in_specs=[pl.BlockSpec((rb, LANES), lambda i: (i, 0)),
                  pl.BlockSpec((1, LANES), lambda i: (0, 0))],
        out_specs=[pl.BlockSpec((rb, LANES), lambda i: (i, 0))] * 2,
        out_shape=[jax.ShapeDtypeStruct((rows, LANES), F32)] * 2,
        compiler_params=_cparams(1),
        name="rope",
    )(pos_rep, inv)
    cos = jnp.tile(cos.reshape(bn, s, half), (1, 1, HEADS_PER_GROUP))
    sin = jnp.tile(sin.reshape(bn, s, half), (1, 1, HEADS_PER_GROUP))
    return cos, sin


def _qk_perm():
    half = B_HEAD_DIM // 2
    perm = np.zeros(B_WIDTH, np.int32)
    for grp in range(N_GROUPS):
        for n in range(GROUP_LANES):
            head = grp * HEADS_PER_GROUP + (n % LANES) // half
            perm[grp * GROUP_LANES + n] = head * B_HEAD_DIM + (n // LANES) * half + n % half
    return perm


def _deinterleave(tok_ref, r4_ref, out1_ref, out4_ref, out16_ref):
    tm = tok_ref.shape[1]
    q4, q16 = tm // 4, tm // 16
    for c in range(SLABS):
        lanes = slice(c * LANES, (c + 1) * LANES)
        out1_ref[0, :, lanes] = tok_ref[c].astype(BF16)
        for r in range(4):
            blk = tok_ref[c, pl.ds(r, q4, stride=4), :]
            out4_ref[0, r, :, lanes] = blk.astype(BF16)
            r4_ref[c, r * q4:(r + 1) * q4, :] = blk
        for r in range(4):
            for r2 in range(4):
                blk = r4_ref[c, pl.ds(r * q4 + r2, q16, stride=4), :]
                out16_ref[0, 4 * r2 + r, :, lanes] = blk.astype(BF16)


def _inproj_kernel(x_ref, mod_ref, g_ref, w_ref, vg_ref, ws_ref, wsb_ref, qg_ref, kg_ref,
                   cos_ref, sin_ref, ya_ref, q1_ref, k1_ref, v1_ref, q4_ref, k4_ref, v4_ref,
                   q16_ref, k16_ref, v16_ref, tok_ref, r4_ref):
    tm = x_ref.shape[1]
    x = x_ref[0]
    h = _modulate(x, g_ref[...], mod_ref[0, 0:1, :], mod_ref[0, 1:2, :]).astype(BF16)

    ua = _gelu(_dot(h, w_ref[:, 0:A_WIDTH]))
    va = _gelu(_dot(h, w_ref[:, A_WIDTH:2 * A_WIDTH]))
    row = lax.broadcasted_iota(jnp.int32, (CHUNK, CHUNK), 0)
    col = lax.broadcasted_iota(jnp.int32, (CHUNK, CHUNK), 1)
    causal = row >= col
    for g in range(A_GROUPS):
        lanes = slice(g * A_GROUP_DIM, (g + 1) * A_GROUP_DIM)
        vg = va[:, lanes]
        mu = jnp.mean(vg, axis=-1, keepdims=True)
        var = jnp.mean(jnp.square(vg - mu), axis=-1, keepdims=True)
        vn = ((vg - mu) * lax.rsqrt(var + EPS) * vg_ref[:, lanes]).astype(BF16)
        ws = jnp.where(causal, ws_ref[g], 0.0).astype(BF16)
        bias = wsb_ref[:, g:g + 1]
        for c in range(tm // CHUNK):
            rows = slice(c * CHUNK, (c + 1) * CHUNK)
            f = _dot(ws, vn[rows, :]) + bias
            ya_ref[0, rows, lanes] = (ua[rows, lanes] * f).astype(BF16)

    seg_r = lax.broadcasted_iota(jnp.int32, (LANES, LANES), 0) // (B_HEAD_DIM // 2)
    seg_c = lax.broadcasted_iota(jnp.int32, (LANES, LANES), 1) // (B_HEAD_DIM // 2)
    seg_ones = jnp.where(seg_r == seg_c, 1.0, 0.0).astype(BF16)
    cos = cos_ref[0]
    sin = sin_ref[0]

    def norm_rot(base, gain_ref, tok, out_scale):
        z = _dot(h, w_ref[:, base:base + B_WIDTH])
        for grp in range(N_GROUPS):
            lo = grp * GROUP_LANES
            z1 = z[:, lo:lo + LANES]
            z2 = z[:, lo + LANES:lo + 2 * LANES]
            ss = _split_dot(z1 * z1 + z2 * z2, seg_ones)
            inv = lax.rsqrt(ss * (1.0 / B_HEAD_DIM) + EPS)
            a1 = z1 * inv * gain_ref[:, lo:lo + LANES]
            a2 = z2 * inv * gain_ref[:, lo + LANES:lo + 2 * LANES]
            tok[2 * grp] = (a1 * cos - a2 * sin) * out_scale
            tok[2 * grp + 1] = (a2 * cos + a1 * sin) * out_scale

    norm_rot(2 * A_WIDTH, qg_ref, tok_ref.at[0], B_HEAD_DIM ** -0.5)
    _deinterleave(tok_ref.at[0], r4_ref.at[0], q1_ref, q4_ref, q16_ref)
    norm_rot(2 * A_WIDTH + B_WIDTH, kg_ref, tok_ref.at[1], 1.0)
    _deinterleave(tok_ref.at[1], r4_ref.at[1], k1_ref, k4_ref, k16_ref)
    v = _dot(h, w_ref[:, 2 * A_WIDTH + 2 * B_WIDTH:])
    for c in range(SLABS):
        tok_ref[2, c] = v[:, c * LANES:(c + 1) * LANES]
    _deinterleave(tok_ref.at[2], r4_ref.at[2], v1_ref, v4_ref, v16_ref)


def _inproj(x, mod, norm_g, w_in, a_vnorm_g, a_spatial_w, a_spatial_b, q_g, k_g, cos, sin):
    bn, s, d = x.shape
    tm = TOKEN_TILE
    perm = _qk_perm()
    qs = 2 * A_WIDTH
    w = jnp.concatenate([w_in[:, :qs], w_in[:, qs:qs + B_WIDTH][:, perm],
                         w_in[:, qs + B_WIDTH:qs + 2 * B_WIDTH][:, perm],
                         w_in[:, qs + 2 * B_WIDTH:]], axis=1).astype(BF16)
    half = B_HEAD_DIM // 2

    def gain_row(g):
        grp = jnp.concatenate([jnp.tile(g[:half], HEADS_PER_GROUP), jnp.tile(g[half:], HEADS_PER_GROUP)])
        return jnp.tile(grp, N_GROUPS).reshape(1, B_WIDTH)

    tok = lambda width: pl.BlockSpec((1, tm, width), lambda b, i: (b, i, 0))
    res = lambda dil: pl.BlockSpec((1, dil, tm // dil, B_WIDTH), lambda b, i: (b, 0, i, 0))
    sds1 = jax.ShapeDtypeStruct((bn, s, B_WIDTH), BF16)
    sds = lambda dil: jax.ShapeDtypeStruct((bn, dil, s // dil, B_WIDTH), BF16)
    outs = pl.pallas_call(
        _inproj_kernel,
        grid=(bn, s // tm),
        in_specs=[
            tok(d),
            pl.BlockSpec((1, 6, d), lambda b, i: (b, 0, 0)),
            _const_spec((1, d)),
            _const_spec(w.shape),
            _const_spec((1, A_WIDTH)),
            _const_spec((A_GROUPS, CHUNK, CHUNK)),
            _const_spec((CHUNK, A_GROUPS)),
            _const_spec((1, B_WIDTH)),
            _const_spec((1, B_WIDTH)),
            tok(LANES),
            tok(LANES),
        ],
        out_specs=[tok(A_WIDTH)] + [tok(B_WIDTH)] * 3 + [res(4)] * 3 + [res(16)] * 3,
        out_shape=[jax.ShapeDtypeStruct((bn, s, A_WIDTH), BF16)] + [sds1] * 3 + [sds(4)] * 3 + [sds(16)] * 3,
        scratch_shapes=[pltpu.VMEM((3, SLABS, tm, LANES), F32), pltpu.VMEM((3, SLABS, tm, LANES), F32)],
        compiler_params=_cparams(2),
        name="inproj",
    )(x, mod, norm_g.reshape(1, d), w, a_vnorm_g.reshape(1, A_WIDTH), a_spatial_w,
      a_spatial_b.T, gain_row(q_g), gain_row(k_g), cos, sin)
    ya = outs[0]
    qkv = {1: [t.reshape(bn, 1, s, B_WIDTH) for t in outs[1:4]], 4: outs[4:7], 16: outs[7:10]}
    return ya, qkv


def _attn_kernel(qmask_ref, vmask_ref, q_ref, kp_ref, kc_ref, vp_ref, vc_ref, o_ref, lse_ref):
    tq = q_ref.shape[0]
    nsub = tq // WINDOW_KEYS
    n = pl.program_id(2)
    qi = lax.broadcasted_iota(jnp.int32, (WINDOW_KEYS, 2 * WINDOW_KEYS), 0)
    kj = lax.broadcasted_iota(jnp.int32, (WINDOW_KEYS, 2 * WINDOW_KEYS), 1)
    dist = qi + WINDOW_KEYS - kj
    band = (dist >= 0) & (dist <= WINDOW_KEYS)
    first_mask = band & ((kj >= WINDOW_KEYS) | (n > 0))
    vlane = lax.broadcasted_iota(jnp.int32, (WINDOW_KEYS, GROUP_LANES), 1) // B_HEAD_DIM

    def window(cur_ref, prev_ref, i):
        rows = slice(i * WINDOW_KEYS, (i + 1) * WINDOW_KEYS)
        prev = prev_ref[...] if i == 0 else cur_ref[(i - 1) * WINDOW_KEYS:i * WINDOW_KEYS, :]
        return jnp.concatenate([prev, cur_ref[rows, :]], axis=0)

    def scores(i):
        rows = slice(i * WINDOW_KEYS, (i + 1) * WINDOW_KEYS)
        kk = window(kc_ref, kp_ref, i)
        q = q_ref[rows, :]
        mask = first_mask if i == 0 else band
        out = []
        for grp in range(N_GROUPS):
            lanes = slice(grp * GROUP_LANES, (grp + 1) * GROUP_LANES)
            qg, kg = q[:, lanes], kk[:, lanes]
            for hh in range(HEADS_PER_GROUP):
                qm = jnp.where(qmask_ref[hh] > 0, qg, jnp.zeros_like(qg))
                sc = lax.dot_general(qm, kg, (((1,), (1,)), ((), ())), preferred_element_type=F32)
                out.append(jnp.where(mask, sc, NEG))
        return out

    def finish(i, scs):
        rows = slice(i * WINDOW_KEYS, (i + 1) * WINDOW_KEYS)
        vv = window(vc_ref, vp_ref, i)
        head_sel = vmask_ref[...]
        for grp in range(N_GROUPS):
            lanes = slice(grp * GROUP_LANES, (grp + 1) * GROUP_LANES)
            vg = vv[:, lanes]
            v_bd = jnp.concatenate([vg] * HEADS_PER_GROUP, axis=0)
            v_bd = jnp.where(head_sel > 0, v_bd, jnp.zeros_like(v_bd))
            ps, ms = [], []
            for hh in range(HEADS_PER_GROUP):
                sc = scs[grp * HEADS_PER_GROUP + hh]
                m = jnp.max(sc, axis=-1, keepdims=True)
                ps.append(jnp.exp(sc - m).astype(BF16))
                ms.append(m)
            p_cat = jnp.concatenate(ps, axis=1)
            pv = _dot(p_cat, v_bd)
            den = _dot(p_cat, head_sel)
            m_cat = jnp.broadcast_to(ms[-1], (WINDOW_KEYS, GROUP_LANES))
            for hh in range(HEADS_PER_GROUP - 1):
                m_cat = jnp.where(vlane == hh, ms[hh], m_cat)
            o_ref[rows, lanes] = (pv / den).astype(BF16)
            lse_ref[rows, lanes] = m_cat + jnp.log(den)

    pending = scores(0)
    for i in range(nsub):
        nxt = scores(i + 1) if i + 1 < nsub else None
        finish(i, pending)
        pending = nxt


def _attn_branch(q, k, v, tq):
    bn, dilation, sub_len, w = q.shape
    blocks_per_step = tq // WINDOW_KEYS
    cur = pl.BlockSpec((None, None, tq, w), lambda b, r, n: (b, r, n, 0))
    prev = pl.BlockSpec((None, None, WINDOW_KEYS, w),
                        lambda b, r, n: (b, r, jnp.maximum(n * blocks_per_step - 1, 0), 0))
    half = B_HEAD_DIM // 2
    lane = np.arange(GROUP_LANES)
    qmask = np.stack([np.broadcast_to(((lane % LANES) // half == hh), (WINDOW_KEYS, GROUP_LANES))
                      for hh in range(HEADS_PER_GROUP)]).astype(np.float32)
    vmask = np.concatenate([np.broadcast_to((lane // B_HEAD_DIM == hh), (2 * WINDOW_KEYS, GROUP_LANES))
                            for hh in range(HEADS_PER_GROUP)]).astype(np.float32)
    return pl.pallas_call(
        _attn_kernel,
        grid=(bn, dilation, sub_len // tq),
        in_specs=[_const_spec(qmask.shape), _const_spec(vmask.shape), cur, prev, cur, prev, cur],
        out_specs=[cur, cur],
        out_shape=[jax.ShapeDtypeStruct(q.shape, BF16), jax.ShapeDtypeStruct(q.shape, F32)],
        compiler_params=_cparams(3),
        name=f"attn_d{dilation}",
    )(jnp.asarray(qmask, BF16), jnp.asarray(vmask, BF16), q, k, k, v, v)


def _to_token_batch(x_ref, pad_ref, tb_ref):
    nb, nt, d = x_ref.shape
    for c in range(d // LANES):
        lanes = slice(c * LANES, (c + 1) * LANES)
        for b in range(nb):
            pad_ref[c, b * TB_PITCH:b * TB_PITCH + nt, :] = x_ref[b, :, lanes]
        for t in range(nt):
            tb_ref[t * nb:(t + 1) * nb, lanes] = pad_ref[c, pl.ds(t, nb, stride=TB_PITCH), :]


def _from_token_batch(tb_ref, pad_ref, out_ref):
    nb, nt, d = out_ref.shape
    for c in range(d // LANES):
        lanes = slice(c * LANES, (c + 1) * LANES)
        for t in range(nt):
            pad_ref[c, pl.ds(t, nb, stride=TB_PITCH), :] = tb_ref[t * nb:(t + 1) * nb, lanes]
        for b in range(nb):
            out_ref[b, :, lanes] = pad_ref[c, b * TB_PITCH:b * TB_PITCH + nt, :]


def _ffn(x1, nb, mod_ref, g_ref, up_ref, dww_ref, dwb_ref, down_ref, act_ref, carry_ref):
    rows = x1.shape[0]
    rep = lambda k: jnp.tile(mod_ref[k], (rows // nb, 1))
    h = _modulate(x1, g_ref[...], rep(3), rep(4)).astype(BF16)

    def up(c):
        return (_dot(h, up_ref[:, c * FFN_CHUNK:(c + 1) * FFN_CHUNK]),
                _dot(h, up_ref[:, FFN_DIM + c * FFN_CHUNK:FFN_DIM + (c + 1) * FFN_CHUNK]))

    def conv(z, col0):
        cols = slice(col0, col0 + FFN_CHUNK)
        hist = carry_ref[:, cols]
        carry_ref[:, cols] = z[rows - 2 * nb:, :]
        z1 = jnp.concatenate([hist[nb:], z[:rows - nb]], axis=0)
        z2 = jnp.concatenate([hist, z[:rows - 2 * nb]], axis=0)
        return (z * dww_ref[2:3, cols] + z1 * dww_ref[1:2, cols] + z2 * dww_ref[0:1, cols]
                + dwb_ref[:, cols])

    nch = FFN_DIM // FFN_CHUNK
    nxt = up(0)
    for c in range(nch):
        za, zb = nxt
        if c + 1 < nch:
            nxt = up(c + 1)
        a = conv(za, c * FFN_CHUNK)
        b = conv(zb, FFN_DIM + c * FFN_CHUNK)
        act_ref[:, c * FFN_CHUNK:(c + 1) * FFN_CHUNK] = (a * _sigmoid(a) * b).astype(BF16)
    return x1 + rep(5) * _dot(act_ref[...], down_ref[...])


def _ffn_weights(up_w, dw_w, dw_b, down_w):
    return up_w.astype(BF16), dw_w, dw_b.reshape(1, 2 * FFN_DIM), down_w.astype(BF16)


def _ffn_specs():
    return [_const_spec((D_MODEL, 2 * FFN_DIM)), _const_spec((FFN_CONV_WIDTH, 2 * FFN_DIM)),
            _const_spec((1, 2 * FFN_DIM)), _const_spec((FFN_DIM, D_MODEL))]


def _ffn_scratch(rows, nb):
    return [pltpu.VMEM((rows, FFN_DIM), BF16), pltpu.VMEM(((FFN_CONV_WIDTH - 1) * nb, 2 * FFN_DIM), F32)]


def _pad_scratch(nb, d):
    return pltpu.VMEM((d // LANES, nb * TB_PITCH, LANES), F32)


def _ffn0_kernel(x_ref, mod_ref, g_ref, up_ref, dww_ref, dwb_ref, down_ref, out_ref,
                 pad_ref, tb_ref, act_ref, carry_ref):
    nb = x_ref.shape[0]

    @pl.when(pl.program_id(0) == 0)
    def _():
        carry_ref[...] = jnp.zeros_like(carry_ref)

    _to_token_batch(x_ref, pad_ref, tb_ref)
    y = _ffn(tb_ref[...], nb, mod_ref, g_ref, up_ref, dww_ref, dwb_ref, down_ref, act_ref, carry_ref)
    out_ref[...] = y.reshape(out_ref.shape)


def _ffn0(x, mod_t, norm_g, ffn_w):
    bn, s, d = x.shape
    nt = TB_TOKENS
    rows = nt * bn
    return pl.pallas_call(
        _ffn0_kernel,
        grid=(s // nt,),
        in_specs=[pl.BlockSpec((bn, nt, d), lambda i: (0, i, 0)), _const_spec((6, bn, d)),
                  _const_spec((1, d))] + _ffn_specs(),
        out_specs=pl.BlockSpec((nt, bn, d), lambda i: (i, 0, 0)),
        out_shape=jax.ShapeDtypeStruct((s, bn, d), F32),
        scratch_shapes=[_pad_scratch(bn, d), pltpu.VMEM((rows, d), F32)] + _ffn_scratch(rows, bn),
        compiler_params=_cparams(1),
        name="ffn0",
    )(x, mod_t, norm_g.reshape(1, d), *ffn_w)


def _interleave4(src_ref, dst_ref):
    q4 = src_ref.shape[1]
    for c in range(SLABS):
        for r in range(4):
            dst_ref[c, pl.ds(r, q4, stride=4), :] = src_ref[r, :, c * LANES:(c + 1) * LANES].astype(F32)


def _interleave16(src_ref, mid_ref, dst_ref):
    q16 = src_ref.shape[1]
    q4 = 4 * q16
    for c in range(SLABS):
        for r in range(4):
            for r2 in range(4):
                mid_ref[c, pl.ds(r * q4 + r2, q16, stride=4), :] = (
                    src_ref[4 * r2 + r, :, c * LANES:(c + 1) * LANES].astype(F32))
        for r in range(4):
            dst_ref[c, pl.ds(r, q4, stride=4), :] = mid_ref[c, r * q4:(r + 1) * q4, :]


def _slabs(ref):
    return jnp.concatenate([ref[c] for c in range(SLABS)], axis=1)


def _mix_kernel(x_ref, mod_ref, ya_ref, o1_ref, l1_ref, o4_ref, l4_ref, o16_ref, l16_ref,
                wout_ref, out_ref, il_ref, mid_ref):
    _interleave4(o4_ref, il_ref.at[0])
    _interleave4(l4_ref, il_ref.at[1])
    _interleave16(o16_ref, mid_ref.at[0], il_ref.at[2])
    _interleave16(l16_ref, mid_ref.at[1], il_ref.at[3])
    o1, l1 = o1_ref[...].astype(F32), l1_ref[...]
    o2, l2 = _slabs(il_ref.at[0]), _slabs(il_ref.at[1])
    o3, l3 = _slabs(il_ref.at[2]), _slabs(il_ref.at[3])
    mx = jnp.maximum(l1, jnp.maximum(l2, l3))
    e1, e2, e3 = jnp.exp(l1 - mx), jnp.exp(l2 - mx), jnp.exp(l3 - mx)
    yb = (e1 * o1 + e2 * o2 + e3 * o3) / (e1 + e2 + e3)
    y = _dot(ya_ref[0], wout_ref[0:A_WIDTH, :]) + _dot(yb.astype(BF16), wout_ref[A_WIDTH:, :])
    out_ref[0] = x_ref[0] + mod_ref[0, 2:3, :] * y


def _mix(x, mod, ya, branches, w_out):
    bn, s, d = x.shape
    tm = TOKEN_TILE
    tok = lambda width: pl.BlockSpec((1, tm, width), lambda b, i: (b, i, 0))
    res = lambda dil: pl.BlockSpec((None, dil, tm // dil, B_WIDTH), lambda b, i: (b, 0, i, 0))
    branch_specs = [pl.BlockSpec((None, None, tm, B_WIDTH), lambda b, i: (b, 0, i, 0))] * 2
    branch_specs += [res(4)] * 2 + [res(16)] * 2
    return pl.pallas_call(
        _mix_kernel,
        grid=(bn, s // tm),
        in_specs=[tok(d), pl.BlockSpec((1, 6, d), lambda b, i: (b, 0, 0)), tok(A_WIDTH)] + branch_specs
                 + [_const_spec((A_WIDTH + B_WIDTH, d))],
        out_specs=tok(d),
        out_shape=jax.ShapeDtypeStruct((bn, s, d), F32),
        scratch_shapes=[pltpu.VMEM((4, SLABS, tm, LANES), F32), pltpu.VMEM((2, SLABS, tm, LANES), F32)],
        compiler_params=_cparams(2),
        name="mix",
    )(x, mod, ya, *branches, w_out.astype(BF16))


def _convffn_kernel(x_ref, mod_ref, gm_ref, pw1_ref, pw1b_ref, dw_ref, dwb_ref, lng_ref, lnb_ref,
                    pw2_ref, pw2b_ref, g_ref, up_ref, fdww_ref, fdwb_ref, down_ref, out_ref,
                    ybuf_ref, cv_ref, pad_ref, tb_ref, act_ref, carry_ref):
    nt, nb, d = x_ref.shape
    rows = nt * nb
    hist = (CONV_WIDTH - 1) * nb

    @pl.when(pl.program_id(0) == 0)
    def _():
        carry_ref[...] = jnp.zeros_like(carry_ref)
        ybuf_ref[0:hist, :] = jnp.zeros((hist, d), F32)

    rep = lambda k: jnp.tile(mod_ref[k], (nt, 1))
    x = x_ref[...].reshape(rows, d)
    h = _modulate(x, gm_ref[...], rep(0), rep(1)).astype(BF16)
    a = _dot(h, pw1_ref[:, 0:d]) + pw1b_ref[:, 0:d]
    gate = _dot(h, pw1_ref[:, d:]) + pw1b_ref[:, d:]
    ybuf_ref[hist:hist + rows, :] = a * _sigmoid(gate)

    rb, lb = CONV_ROW_BLOCK, CONV_LANE_BLOCK
    for r in range(rows // rb):
        for l in range(d // lb):
            lanes = slice(l * lb, (l + 1) * lb)
            acc = jnp.zeros((rb, lb), F32) + dwb_ref[:, lanes]
            for k in range(CONV_WIDTH):
                start = r * rb + k * nb
                acc = acc + ybuf_ref[start:start + rb, lanes] * dw_ref[k:k + 1, lanes]
            cv_ref[r * rb:(r + 1) * rb, lanes] = acc
    ybuf_ref[0:hist, :] = ybuf_ref[rows:rows + hist, :]

    cv = cv_ref[...]
    mu = jnp.mean(cv, axis=-1, keepdims=True)
    var = jnp.mean(jnp.square(cv - mu), axis=-1, keepdims=True)
    yn = (cv - mu) * lax.rsqrt(var + EPS) * lng_ref[...] + lnb_ref[...]
    act = (yn * _sigmoid(yn)).astype(BF16)
    y = _dot(act, pw2_ref[...]) + pw2b_ref[...]
    x1 = x + rep(2) * y
    tb_ref[...] = _ffn(x1, nb, mod_ref, g_ref, up_ref, fdww_ref, fdwb_ref, down_ref, act_ref, carry_ref)
    _from_token_batch(tb_ref, pad_ref, out_ref)


def _convffn(x_tb, mod_t, norm_mix_g, pw1_w, pw1_b, dw_w, dw_b, ln_g, ln_b, pw2_w, pw2_b, norm_ffn_g, ffn_w):
    s, bn, d = x_tb.shape
    nt = TB_TOKENS
    rows = nt * bn
    row = lambda t: t.reshape(1, -1)
    return pl.pallas_call(
        _convffn_kernel,
        grid=(s // nt,),
        in_specs=[pl.BlockSpec((nt, bn, d), lambda i: (i, 0, 0)), _const_spec((6, bn, d)), _const_spec((1, d)),
                  _const_spec((d, 2 * d)), _const_spec((1, 2 * d)), _const_spec((CONV_WIDTH, d)),
                  _const_spec((1, d)), _const_spec((1, d)), _const_spec((1, d)),
                  _const_spec((d, d)), _const_spec((1, d)), _const_spec((1, d))] + _ffn_specs(),
        out_specs=pl.BlockSpec((bn, nt, d), lambda i: (0, i, 0)),
        out_shape=jax.ShapeDtypeStruct((bn, s, d), F32),
        scratch_shapes=[pltpu.VMEM((rows + (CONV_WIDTH - 1) * bn, d), F32), pltpu.VMEM((rows, d), F32),
                        _pad_scratch(bn, d), pltpu.VMEM((rows, d), F32)] + _ffn_scratch(rows, bn),
        compiler_params=_cparams(1),
        name="convffn",
    )(x_tb, mod_t, row(norm_mix_g), pw1_w.astype(BF16), row(pw1_b), dw_w, row(dw_b), row(ln_g), row(ln_b),
      pw2_w.astype(BF16), row(pw2_b), row(norm_ffn_g), *ffn_w)


def kernel(x, c, positions, ada_w, ada_b, norm_mix_g, norm_ffn_g, ab_w_in, a_vnorm_g, a_spatial_w, a_spatial_b, b_q_norm_g, b_k_norm_g, ab_w_out, conv_pw1_w, conv_pw1_b, conv_dw_w, conv_dw_b, conv_ln_g, conv_ln_b, conv_pw2_w, conv_pw2_b, ffn_up_w, ffn_dw_w, ffn_dw_b, ffn_down_w):
    bn, s, d = x.shape
    mod = _ada(c, ada_w, ada_b).reshape(ada_w.shape[0], bn, 6, d)
    cos, sin = _rope_tables(positions)

    ya, qkv = _inproj(x, mod[0], norm_mix_g[0], ab_w_in[0], a_vnorm_g[0], a_spatial_w[0],
                      a_spatial_b[0], b_q_norm_g[0], b_k_norm_g[0], cos, sin)
    branches = []
    for window, dilation in DILATED_PATTERNS:
        assert window // dilation == WINDOW_KEYS
        tq = min(512, s // dilation)
        branches += _attn_branch(*qkv[dilation], tq)
    x = _mix(x, mod[0], ya, branches, ab_w_out[0])
    mod_t = jnp.swapaxes(mod, 1, 2)
    ffn0 = _ffn_weights(ffn_up_w[0], ffn_dw_w[0], ffn_dw_b[0], ffn_down_w[0])
    x_tb = _ffn0(x, mod_t[0], norm_ffn_g[0], ffn0)

    ffn1 = _ffn_weights(ffn_up_w[1], ffn_dw_w[1], ffn_dw_b[1], ffn_down_w[1])
    return _convffn(x_tb, mod_t[1], norm_mix_g[1], conv_pw1_w[0], conv_pw1_b[0], conv_dw_w[0], conv_dw_b[0],
                    conv_ln_g[0], conv_ln_b[0], conv_pw2_w[0], conv_pw2_b[0], norm_ffn_g[1], ffn1)
```

```python
import math

import numpy as np
import jax
import jax.numpy as jnp
from jax import lax
from jax.experimental import pallas as pl
from jax.experimental.pallas import tpu as pltpu

F32 = jnp.float32
BF16 = jnp.bfloat16

D_MODEL = 1024
A_WIDTH = 512
A_GROUPS = 4
A_GROUP_DIM = 128
CHUNK = 128
B_WIDTH = 512
B_HEAD_DIM = 64
B_HEADS = 8
HEADS_PER_GROUP = 4
GROUP_LANES = HEADS_PER_GROUP * B_HEAD_DIM
N_GROUPS = B_HEADS // HEADS_PER_GROUP
DILATED_PATTERNS = ((128, 1), (512, 4), (2048, 16))
WINDOW_KEYS = 128
ROPE_THETA = 10000.0
CONV_WIDTH = 31
CONV_ROW_BLOCK = 64
CONV_LANE_BLOCK = 256
GLU_CHUNK = 256
FFN_DIM = 2816
FFN_CONV_WIDTH = 3
FFN_CHUNK = 256
ROPE_ROW_BLOCK = 1024
ADA_COL_BLOCK = 1536
LSE_REP = 16
TB_TOKENS = 64
TB_PITCH = 72
EPS = 1e-6
NEG = -1e30

LANES = 128
SLABS = B_WIDTH // LANES
TOKEN_TILE = 512
V7X_VMEM_LIMIT_BYTES = 56 * 1024 * 1024


def _cparams(n_axes):
    return pltpu.CompilerParams(
        dimension_semantics=("arbitrary",) * n_axes,
        vmem_limit_bytes=V7X_VMEM_LIMIT_BYTES)


def _const_spec(shape):
    nd = len(shape)
    return pl.BlockSpec(shape, lambda *_: (0,) * nd, pipeline_mode=pl.Buffered(1))


def _dot(a, b):
    return jnp.dot(a, b, preferred_element_type=F32)


def _split_dot(a, b):
    hi = a.astype(BF16)
    lo = (a - hi.astype(F32)).astype(BF16)
    return _dot(hi, b) + _dot(lo, b)


def _sigmoid(x):
    return jax.nn.sigmoid(x)


def _gelu(x):
    return 0.5 * x * (1.0 + lax.erf(x * (1.0 / math.sqrt(2.0))))


def _modulate(x, g, shift, scale):
    ms = jnp.mean(x * x, axis=-1, keepdims=True)
    return (x * lax.rsqrt(ms + EPS) * g) * (1.0 + scale) + shift


def _ada_kernel(c_ref, w_ref, b_ref, o_ref):
    c = c_ref[...]
    act = (c * _sigmoid(c)).astype(BF16)
    o_ref[0] = _dot(act, w_ref[0].astype(BF16)) + b_ref[0]


def _ada(c, ada_w, ada_b):
    depth, d, n = ada_w.shape
    bn = c.shape[0]
    nb = ADA_COL_BLOCK
    return pl.pallas_call(
        _ada_kernel,
        grid=(depth, n // nb),
        in_specs=[
            pl.BlockSpec((bn, d), lambda l, j: (0, 0)),
            pl.BlockSpec((1, d, nb), lambda l, j: (l, 0, j)),
            pl.BlockSpec((1, 1, nb), lambda l, j: (l, 0, j)),
        ],
        out_specs=pl.BlockSpec((1, bn, nb), lambda l, j: (l, 0, j)),
        out_shape=jax.ShapeDtypeStruct((depth, bn, n), F32),
        compiler_params=_cparams(2),
        name="ada",
    )(c, ada_w, ada_b.reshape(depth, 1, n))


def _rope_kernel(pos_ref, inv_ref, cos_ref, sin_ref):
    rb = pos_ref.shape[0]
    half = B_HEAD_DIM // 2
    per_row = LANES // half
    ang = pos_ref[...].astype(F32) * inv_ref[...]
    cos, sin = jnp.cos(ang), jnp.sin(ang)
    src = lax.broadcasted_iota(jnp.int32, (LANES, LANES), 0)
    dst = lax.broadcasted_iota(jnp.int32, (LANES, LANES), 1)
    for j in range(per_row):
        spread = jnp.where((src // half == j) & (src % half == dst % half), 1.0, 0.0).astype(BF16)
        cos_ref[pl.ds(j, rb, stride=per_row), :] = _split_dot(cos, spread)
        sin_ref[pl.ds(j, rb, stride=per_row), :] = _split_dot(sin, spread)


def _rope_tables(positions):
    bn, s = positions.shape
    half = B_HEAD_DIM // 2
    inv_freq = 1.0 / (ROPE_THETA ** (jnp.arange(0, B_HEAD_DIM, 2, dtype=F32) / B_HEAD_DIM))
    per_row = LANES // half
    rows = bn * s // per_row
    pos_rep = jnp.repeat(positions.reshape(rows, per_row), half, axis=1)
    inv = jnp.tile(inv_freq, per_row).reshape(1, LANES)
    rb = min(rows, ROPE_ROW_BLOCK)
    cos, sin = pl.pallas_call(
        _rope_kernel,
        grid=(rows // rb,),
        in_specs=[pl.BlockSpec((rb, LANES), lambda i: (i, 0)),
                  pl.BlockSpec((1, LANES), lambda i: (0, 0))],
        out_specs=[pl.BlockSpec((per_row * rb, LANES), lambda i: (i, 0))] * 2,
        out_shape=[jax.ShapeDtypeStruct((bn * s, LANES), F32)] * 2,
        compiler_params=_cparams(1),
        name="rope",
    )(pos_rep, inv)
    return cos.reshape(bn, s, LANES), sin.reshape(bn, s, LANES)


def _qk_perm():
    half = B_HEAD_DIM // 2
    perm = np.zeros(B_WIDTH, np.int32)
    for grp in range(N_GROUPS):
        for n in range(GROUP_LANES):
            head = grp * HEADS_PER_GROUP + (n % LANES) // half
            perm[grp * GROUP_LANES + n] = head * B_HEAD_DIM + (n // LANES) * half + n % half
    return perm


def _deinterleave(tok_ref, r4_ref, out1_ref, out4_ref, out16_ref):
    tm = tok_ref.shape[1]
    q4, q16 = tm // 4, tm // 16
    for c in range(SLABS):
        lanes = slice(c * LANES, (c + 1) * LANES)
        out1_ref[0, :, lanes] = tok_ref[c].astype(BF16)
        for r in range(4):
            blk = tok_ref[c, pl.ds(r, q4, stride=4), :]
            out4_ref[0, r, :, lanes] = blk.astype(BF16)
            r4_ref[c, r * q4:(r + 1) * q4, :] = blk
        for r in range(4):
            for r2 in range(4):
                blk = r4_ref[c, pl.ds(r * q4 + r2, q16, stride=4), :]
                out16_ref[0, 4 * r2 + r, :, lanes] = blk.astype(BF16)


def _inproj_kernel(x_ref, mod_ref, g_ref, w_ref, vg_ref, ws_ref, wsb_ref, qg_ref, kg_ref,
                   cos_ref, sin_ref, ya_ref, q1_ref, k1_ref, v1_ref, q4_ref, k4_ref, v4_ref,
                   q16_ref, k16_ref, v16_ref, tok_ref, r4_ref):
    tm = x_ref.shape[1]
    x = x_ref[0]
    h = _modulate(x, g_ref[...], mod_ref[0, 0:1, :], mod_ref[0, 1:2, :]).astype(BF16)

    ua = _gelu(_dot(h, w_ref[:, 0:A_WIDTH]))
    va = _gelu(_dot(h, w_ref[:, A_WIDTH:2 * A_WIDTH]))
    row = lax.broadcasted_iota(jnp.int32, (CHUNK, CHUNK), 0)
    col = lax.broadcasted_iota(jnp.int32, (CHUNK, CHUNK), 1)
    causal = row >= col
    for g in range(A_GROUPS):
        lanes = slice(g * A_GROUP_DIM, (g + 1) * A_GROUP_DIM)
        vg = va[:, lanes]
        mu = jnp.mean(vg, axis=-1, keepdims=True)
        var = jnp.mean(jnp.square(vg - mu), axis=-1, keepdims=True)
        vn = ((vg - mu) * lax.rsqrt(var + EPS) * vg_ref[:, lanes]).astype(BF16)
        ws = jnp.where(causal, ws_ref[g], 0.0).astype(BF16)
        bias = wsb_ref[:, g:g + 1]
        for c in range(tm // CHUNK):
            rows = slice(c * CHUNK, (c + 1) * CHUNK)
            f = _dot(ws, vn[rows, :]) + bias
            ya_ref[0, rows, lanes] = (ua[rows, lanes] * f).astype(BF16)

    seg_r = lax.broadcasted_iota(jnp.int32, (LANES, LANES), 0) // (B_HEAD_DIM // 2)
    seg_c = lax.broadcasted_iota(jnp.int32, (LANES, LANES), 1) // (B_HEAD_DIM // 2)
    seg_ones = jnp.where(seg_r == seg_c, 1.0, 0.0).astype(BF16)
    cos = cos_ref[0]
    sin = sin_ref[0]

    def norm_rot(base, gain_ref, tok, out_scale):
        z = _dot(h, w_ref[:, base:base + B_WIDTH])
        for grp in range(N_GROUPS):
            lo = grp * GROUP_LANES
            z1 = z[:, lo:lo + LANES]
            z2 = z[:, lo + LANES:lo + 2 * LANES]
            ss = _split_dot(z1 * z1 + z2 * z2, seg_ones)
            inv = lax.rsqrt(ss * (1.0 / B_HEAD_DIM) + EPS)
            a1 = z1 * inv * gain_ref[:, lo:lo + LANES]
            a2 = z2 * inv * gain_ref[:, lo + LANES:lo + 2 * LANES]
            tok[2 * grp] = (a1 * cos - a2 * sin) * out_scale
            tok[2 * grp + 1] = (a2 * cos + a1 * sin) * out_scale

    norm_rot(2 * A_WIDTH, qg_ref, tok_ref.at[0], B_HEAD_DIM ** -0.5)
    _deinterleave(tok_ref.at[0], r4_ref.at[0], q1_ref, q4_ref, q16_ref)
    norm_rot(2 * A_WIDTH + B_WIDTH, kg_ref, tok_ref.at[1], 1.0)
    _deinterleave(tok_ref.at[1], r4_ref.at[1], k1_ref, k4_ref, k16_ref)
    v = _dot(h, w_ref[:, 2 * A_WIDTH + 2 * B_WIDTH:])
    for c in range(SLABS):
        tok_ref[2, c] = v[:, c * LANES:(c + 1) * LANES]
    _deinterleave(tok_ref.at[2], r4_ref.at[2], v1_ref, v4_ref, v16_ref)


def _inproj(x, mod, norm_g, w_in, a_vnorm_g, a_spatial_w, a_spatial_b, q_g, k_g, cos, sin):
    bn, s, d = x.shape
    tm = TOKEN_TILE
    perm = _qk_perm()
    qs = 2 * A_WIDTH
    w = jnp.concatenate([w_in[:, :qs], w_in[:, qs:qs + B_WIDTH][:, perm],
                         w_in[:, qs + B_WIDTH:qs + 2 * B_WIDTH][:, perm],
                         w_in[:, qs + 2 * B_WIDTH:]], axis=1).astype(BF16)
    half = B_HEAD_DIM // 2

    def gain_row(g):
        grp = jnp.concatenate([jnp.tile(g[:half], HEADS_PER_GROUP), jnp.tile(g[half:], HEADS_PER_GROUP)])
        return jnp.tile(grp, N_GROUPS).reshape(1, B_WIDTH)

    tok = lambda width: pl.BlockSpec((1, tm, width), lambda b, i: (b, i, 0))
    res = lambda dil: pl.BlockSpec((1, dil, tm // dil, B_WIDTH), lambda b, i: (b, 0, i, 0))
    sds1 = jax.ShapeDtypeStruct((bn, s, B_WIDTH), BF16)
    sds = lambda dil: jax.ShapeDtypeStruct((bn, dil, s // dil, B_WIDTH), BF16)
    outs = pl.pallas_call(
        _inproj_kernel,
        grid=(bn, s // tm),
        in_specs=[
            tok(d),
            pl.BlockSpec((1, 6, d), lambda b, i: (b, 0, 0)),
            _const_spec((1, d)),
            _const_spec(w.shape),
            _const_spec((1, A_WIDTH)),
            _const_spec((A_GROUPS, CHUNK, CHUNK)),
            _const_spec((CHUNK, A_GROUPS)),
            _const_spec((1, B_WIDTH)),
            _const_spec((1, B_WIDTH)),
            tok(LANES),
            tok(LANES),
        ],
        out_specs=[tok(A_WIDTH)] + [tok(B_WIDTH)] * 3 + [res(4)] * 3 + [res(16)] * 3,
        out_shape=[jax.ShapeDtypeStruct((bn, s, A_WIDTH), BF16)] + [sds1] * 3 + [sds(4)] * 3 + [sds(16)] * 3,
        scratch_shapes=[pltpu.VMEM((3, SLABS, tm, LANES), F32), pltpu.VMEM((3, SLABS, tm, LANES), F32)],
        compiler_params=_cparams(2),
        name="inproj",
    )(x, mod, norm_g.reshape(1, d), w, a_vnorm_g.reshape(1, A_WIDTH), a_spatial_w,
      a_spatial_b.T, gain_row(q_g), gain_row(k_g), cos, sin)
    ya = outs[0]
    qkv = {1: [t.reshape(bn, 1, s, B_WIDTH) for t in outs[1:4]], 4: outs[4:7], 16: outs[7:10]}
    return ya, qkv


def _attn_kernel(qmask_ref, vmask_ref, q_ref, kp_ref, kc_ref, vp_ref, vc_ref, o_ref, lse_ref):
    tq = q_ref.shape[0]
    nsub = tq // WINDOW_KEYS
    n = pl.program_id(2)
    qi = lax.broadcasted_iota(jnp.int32, (WINDOW_KEYS, 2 * WINDOW_KEYS), 0)
    kj = lax.broadcasted_iota(jnp.int32, (WINDOW_KEYS, 2 * WINDOW_KEYS), 1)
    dist = qi + WINDOW_KEYS - kj
    band = (dist >= 0) & (dist <= WINDOW_KEYS)
    first_mask = band & ((kj >= WINDOW_KEYS) | (n > 0))
    vlane = lax.broadcasted_iota(jnp.int32, (WINDOW_KEYS, GROUP_LANES), 1) // B_HEAD_DIM
    lse_pick = (lax.broadcasted_iota(jnp.int32, (WINDOW_KEYS, LANES), 1) % B_HEAD_DIM) // LSE_REP

    def window(cur_ref, prev_ref, i):
        rows = slice(i * WINDOW_KEYS, (i + 1) * WINDOW_KEYS)
        prev = prev_ref[...] if i == 0 else cur_ref[(i - 1) * WINDOW_KEYS:i * WINDOW_KEYS, :]
        return jnp.concatenate([prev, cur_ref[rows, :]], axis=0)

    def scores(i):
        rows = slice(i * WINDOW_KEYS, (i + 1) * WINDOW_KEYS)
        kk = window(kc_ref, kp_ref, i)
        q = q_ref[rows, :]
        mask = first_mask if i == 0 else band
        out = []
        for grp in range(N_GROUPS):
            lanes = slice(grp * GROUP_LANES, (grp + 1) * GROUP_LANES)
            qg, kg = q[:, lanes], kk[:, lanes]
            for hh in range(HEADS_PER_GROUP):
                qm = jnp.where(qmask_ref[hh] > 0, qg, jnp.zeros_like(qg))
                sc = lax.dot_general(qm, kg, (((1,), (1,)), ((), ())), preferred_element_type=F32)
                out.append(jnp.where(mask, sc, NEG))
        return out

    def finish(i, scs):
        rows = slice(i * WINDOW_KEYS, (i + 1) * WINDOW_KEYS)
        vv = window(vc_ref, vp_ref, i)
        head_sel = vmask_ref[...]
        lse_parts = []
        for grp in range(N_GROUPS):
            lanes = slice(grp * GROUP_LANES, (grp + 1) * GROUP_LANES)
            vg = vv[:, lanes]
            v_bd = jnp.concatenate([vg] * HEADS_PER_GROUP, axis=0)
            v_bd = jnp.where(head_sel > 0, v_bd, jnp.zeros_like(v_bd))
            ps, ms = [], []
            for hh in range(HEADS_PER_GROUP):
                sc = scs[grp * HEADS_PER_GROUP + hh]
                m = jnp.max(sc, axis=-1, keepdims=True)
                ps.append(jnp.exp(sc - m).astype(BF16))
                ms.append(m)
            p_cat = jnp.concatenate(ps, axis=1)
            pv = _dot(p_cat, v_bd)
            den = _dot(p_cat, head_sel)
            m_cat = jnp.broadcast_to(ms[-1], (WINDOW_KEYS, GROUP_LANES))
            for hh in range(HEADS_PER_GROUP - 1):
                m_cat = jnp.where(vlane == hh, ms[hh], m_cat)
            o_ref[rows, lanes] = (pv / den).astype(BF16)
            lse = m_cat + jnp.log(den)
            lse_parts += [lse[:, :LANES], lse[:, LANES:]]
        tile = lse_parts[-1]
        for part in range(len(lse_parts) - 1):
            tile = jnp.where(lse_pick == part, lse_parts[part], tile)
        lse_ref[rows, :] = tile

    pending = scores(0)
    for i in range(nsub):
        nxt = scores(i + 1) if i + 1 < nsub else None
        finish(i, pending)
        pending = nxt


def _attn_branch(q, k, v, tq):
    bn, dilation, sub_len, w = q.shape
    blocks_per_step = tq // WINDOW_KEYS
    cur = pl.BlockSpec((None, None, tq, w), lambda b, r, n: (b, r, n, 0))
    prev = pl.BlockSpec((None, None, WINDOW_KEYS, w),
                        lambda b, r, n: (b, r, jnp.maximum(n * blocks_per_step - 1, 0), 0))
    half = B_HEAD_DIM // 2
    lane = np.arange(GROUP_LANES)
    qmask = np.stack([np.broadcast_to(((lane % LANES) // half == hh), (WINDOW_KEYS, GROUP_LANES))
                      for hh in range(HEADS_PER_GROUP)]).astype(np.float32)
    vmask = np.concatenate([np.broadcast_to((lane // B_HEAD_DIM == hh), (2 * WINDOW_KEYS, GROUP_LANES))
                            for hh in range(HEADS_PER_GROUP)]).astype(np.float32)
    return pl.pallas_call(
        _attn_kernel,
        grid=(bn, dilation, sub_len // tq),
        in_specs=[_const_spec(qmask.shape), _const_spec(vmask.shape), cur, prev, cur, prev, cur],
        out_specs=[cur, pl.BlockSpec((None, None, tq, LANES), lambda b, r, n: (b, r, n, 0))],
        out_shape=[jax.ShapeDtypeStruct(q.shape, BF16),
                   jax.ShapeDtypeStruct((bn, dilation, sub_len, LANES), F32)],
        compiler_params=_cparams(3),
        name=f"attn_d{dilation}",
    )(jnp.asarray(qmask, BF16), jnp.asarray(vmask, BF16), q, k, k, v, v)


def _to_token_batch(x_ref, pad_ref, tb_ref):
    nb, nt, d = x_ref.shape
    for c in range(d // LANES):
        lanes = slice(c * LANES, (c + 1) * LANES)
        for b in range(nb):
            pad_ref[c, b * TB_PITCH:b * TB_PITCH + nt, :] = x_ref[b, :, lanes]
        for t in range(nt):
            tb_ref[t * nb:(t + 1) * nb, lanes] = pad_ref[c, pl.ds(t, nb, stride=TB_PITCH), :]


def _from_token_batch(tb_ref, pad_ref, out_ref):
    nb, nt, d = out_ref.shape
    for c in range(d // LANES):
        lanes = slice(c * LANES, (c + 1) * LANES)
        for t in range(nt):
            pad_ref[c, pl.ds(t, nb, stride=TB_PITCH), :] = tb_ref[t * nb:(t + 1) * nb, lanes]
        for b in range(nb):
            out_ref[b, :, lanes] = pad_ref[c, b * TB_PITCH:b * TB_PITCH + nt, :]


def _ffn(x1, nb, mod_ref, g_ref, up_ref, dww_ref, dwb_ref, down_ref, act_ref, carry_ref):
    rows = x1.shape[0]
    rep = lambda k: jnp.tile(mod_ref[k], (rows // nb, 1))
    h = _modulate(x1, g_ref[...], rep(3), rep(4)).astype(BF16)

    def up(c):
        return (_dot(h, up_ref[:, c * FFN_CHUNK:(c + 1) * FFN_CHUNK]),
                _dot(h, up_ref[:, FFN_DIM + c * FFN_CHUNK:FFN_DIM + (c + 1) * FFN_CHUNK]))

    def conv(z, col0):
        cols = slice(col0, col0 + FFN_CHUNK)
        hist = carry_ref[:, cols]
        carry_ref[:, cols] = z[rows - 2 * nb:, :]
        z1 = jnp.concatenate([hist[nb:], z[:rows - nb]], axis=0)
        z2 = jnp.concatenate([hist, z[:rows - 2 * nb]], axis=0)
        return (z * dww_ref[2:3, cols] + z1 * dww_ref[1:2, cols] + z2 * dww_ref[0:1, cols]
                + dwb_ref[:, cols])

    nch = FFN_DIM // FFN_CHUNK
    nxt = up(0)
    for c in range(nch):
        za, zb = nxt
        if c + 1 < nch:
            nxt = up(c + 1)
        a = conv(za, c * FFN_CHUNK)
        b = conv(zb, FFN_DIM + c * FFN_CHUNK)
        act_ref[:, c * FFN_CHUNK:(c + 1) * FFN_CHUNK] = (a * _sigmoid(a) * b).astype(BF16)
    return x1 + rep(5) * _dot(act_ref[...], down_ref[...])


def _ffn_weights(up_w, dw_w, dw_b, down_w):
    return up_w.astype(BF16), dw_w, dw_b.reshape(1, 2 * FFN_DIM), down_w.astype(BF16)


def _ffn_specs():
    return [_const_spec((D_MODEL, 2 * FFN_DIM)), _const_spec((FFN_CONV_WIDTH, 2 * FFN_DIM)),
            _const_spec((1, 2 * FFN_DIM)), _const_spec((FFN_DIM, D_MODEL))]


def _ffn_scratch(rows, nb):
    return [pltpu.VMEM((rows, FFN_DIM), BF16), pltpu.VMEM(((FFN_CONV_WIDTH - 1) * nb, 2 * FFN_DIM), F32)]


def _pad_scratch(nb, d):
    return pltpu.VMEM((d // LANES, nb * TB_PITCH, LANES), F32)


def _ffn0_kernel(x_ref, mod_ref, g_ref, up_ref, dww_ref, dwb_ref, down_ref, out_ref,
                 pad_ref, tb_ref, act_ref, carry_ref):
    nb = x_ref.shape[0]

    @pl.when(pl.program_id(0) == 0)
    def _():
        carry_ref[...] = jnp.zeros_like(carry_ref)

    _to_token_batch(x_ref, pad_ref, tb_ref)
    y = _ffn(tb_ref[...], nb, mod_ref, g_ref, up_ref, dww_ref, dwb_ref, down_ref, act_ref, carry_ref)
    out_ref[...] = y.reshape(out_ref.shape)


def _ffn0(x, mod_t, norm_g, ffn_w):
    bn, s, d = x.shape
    nt = TB_TOKENS
    rows = nt * bn
    return pl.pallas_call(
        _ffn0_kernel,
        grid=(s // nt,),
        in_specs=[pl.BlockSpec((bn, nt, d), lambda i: (0, i, 0)), _const_spec((6, bn, d)),
                  _const_spec((1, d))] + _ffn_specs(),
        out_specs=pl.BlockSpec((nt, bn, d), lambda i: (i, 0, 0)),
        out_shape=jax.ShapeDtypeStruct((s, bn, d), F32),
        scratch_shapes=[_pad_scratch(bn, d), pltpu.VMEM((rows, d), F32)] + _ffn_scratch(rows, bn),
        compiler_params=_cparams(1),
        name="ffn0",
    )(x, mod_t, norm_g.reshape(1, d), *ffn_w)


def _interleave4(src_ref, dst_ref):
    q4 = src_ref.shape[1]
    for c in range(src_ref.shape[2] // LANES):
        for r in range(4):
            dst_ref[c, pl.ds(r, q4, stride=4), :] = src_ref[r, :, c * LANES:(c + 1) * LANES].astype(F32)


def _interleave16(src_ref, mid_ref, dst_ref):
    q16 = src_ref.shape[1]
    q4 = 4 * q16
    for c in range(src_ref.shape[2] // LANES):
        for r in range(4):
            for r2 in range(4):
                mid_ref[c, pl.ds(r * q4 + r2, q16, stride=4), :] = (
                    src_ref[4 * r2 + r, :, c * LANES:(c + 1) * LANES].astype(F32))
        for r in range(4):
            dst_ref[c, pl.ds(r, q4, stride=4), :] = mid_ref[c, r * q4:(r + 1) * q4, :]


def _slabs(ref):
    return jnp.concatenate([ref[c] for c in range(ref.shape[0])], axis=1)


def _lse_lane(head):
    return B_HEAD_DIM * (head % 2) + LSE_REP * (head // 2)


def _mix_kernel(x_ref, mod_ref, ya_ref, o1_ref, l1_ref, o4_ref, l4_ref, o16_ref, l16_ref,
                exp_ref, wout_ref, out_ref, ilo_ref, mido_ref, ill_ref, midl_ref):
    _interleave4(o4_ref, ilo_ref.at[0])
    _interleave16(o16_ref, mido_ref, ilo_ref.at[1])
    _interleave4(l4_ref, ill_ref.at[0])
    _interleave16(l16_ref, midl_ref, ill_ref.at[1])
    l1, l2, l3 = l1_ref[...], ill_ref[0, 0], ill_ref[1, 0]
    mx = jnp.maximum(l1, jnp.maximum(l2, l3))
    e1, e2, e3 = jnp.exp(l1 - mx), jnp.exp(l2 - mx), jnp.exp(l3 - mx)
    tot = e1 + e2 + e3
    expand = exp_ref[...]
    yb = (_split_dot(e1 / tot, expand) * o1_ref[...].astype(F32)
          + _split_dot(e2 / tot, expand) * _slabs(ilo_ref.at[0])
          + _split_dot(e3 / tot, expand) * _slabs(ilo_ref.at[1]))
    y = _dot(ya_ref[0], wout_ref[0:A_WIDTH, :]) + _dot(yb.astype(BF16), wout_ref[A_WIDTH:, :])
    out_ref[0] = x_ref[0] + mod_ref[0, 2:3, :] * y


def _mix(x, mod, ya, branches, w_out):
    bn, s, d = x.shape
    tm = TOKEN_TILE
    tok = lambda width: pl.BlockSpec((1, tm, width), lambda b, i: (b, i, 0))
    branch_specs = []
    for dil in (1, 4, 16):
        for width in (B_WIDTH, LANES):
            if dil == 1:
                branch_specs.append(pl.BlockSpec((None, None, tm, width), lambda b, i: (b, 0, i, 0)))
            else:
                branch_specs.append(pl.BlockSpec((None, dil, tm // dil, width), lambda b, i: (b, 0, i, 0)))
    expand = np.zeros((LANES, B_WIDTH), np.float32)
    for head in range(B_HEADS):
        expand[_lse_lane(head), head * B_HEAD_DIM:(head + 1) * B_HEAD_DIM] = 1.0
    return pl.pallas_call(
        _mix_kernel,
        grid=(bn, s // tm),
        in_specs=[tok(d), pl.BlockSpec((1, 6, d), lambda b, i: (b, 0, 0)), tok(A_WIDTH)] + branch_specs
                 + [_const_spec((LANES, B_WIDTH)), _const_spec((A_WIDTH + B_WIDTH, d))],
        out_specs=tok(d),
        out_shape=jax.ShapeDtypeStruct((bn, s, d), F32),
        scratch_shapes=[pltpu.VMEM((2, SLABS, tm, LANES), F32), pltpu.VMEM((SLABS, tm, LANES), F32),
                        pltpu.VMEM((2, 1, tm, LANES), F32), pltpu.VMEM((1, tm, LANES), F32)],
        compiler_params=_cparams(2),
        name="mix",
    )(x, mod, ya, *branches, jnp.asarray(expand, BF16), w_out.astype(BF16))


def _convffn_kernel(x_ref, mod_ref, gm_ref, pw1_ref, pw1b_ref, dw_ref, dwb_ref, lng_ref, lnb_ref,
                    pw2_ref, pw2b_ref, g_ref, up_ref, fdww_ref, fdwb_ref, down_ref, out_ref,
                    ybuf_ref, cv_ref, pad_ref, tb_ref, act_ref, carry_ref):
    nt, nb, d = x_ref.shape
    rows = nt * nb
    hist = (CONV_WIDTH - 1) * nb

    @pl.when(pl.program_id(0) == 0)
    def _():
        carry_ref[...] = jnp.zeros_like(carry_ref)
        ybuf_ref[0:hist, :] = jnp.zeros((hist, d), F32)

    rep = lambda k: jnp.tile(mod_ref[k], (nt, 1))
    x = x_ref[...].reshape(rows, d)
    h = _modulate(x, gm_ref[...], rep(0), rep(1)).astype(BF16)
    for j in range(d // GLU_CHUNK):
        cols = slice(j * GLU_CHUNK, (j + 1) * GLU_CHUNK)
        gcols = slice(d + j * GLU_CHUNK, d + (j + 1) * GLU_CHUNK)
        a = _dot(h, pw1_ref[:, cols]) + pw1b_ref[:, cols]
        gate = _dot(h, pw1_ref[:, gcols]) + pw1b_ref[:, gcols]
        ybuf_ref[hist:hist + rows, cols] = a * _sigmoid(gate)

    rb, lb = CONV_ROW_BLOCK, CONV_LANE_BLOCK
    for r in range(rows // rb):
        for l in range(d // lb):
            lanes = slice(l * lb, (l + 1) * lb)
            acc = jnp.zeros((rb, lb), F32) + dwb_ref[:, lanes]
            for k in range(CONV_WIDTH):
                start = r * rb + k * nb
                acc = acc + ybuf_ref[start:start + rb, lanes] * dw_ref[k:k + 1, lanes]
            cv_ref[r * rb:(r + 1) * rb, lanes] = acc
    ybuf_ref[0:hist, :] = ybuf_ref[rows:rows + hist, :]

    cv = cv_ref[...]
    mu = jnp.mean(cv, axis=-1, keepdims=True)
    var = jnp.mean(jnp.square(cv - mu), axis=-1, keepdims=True)
    yn = (cv - mu) * lax.rsqrt(var + EPS) * lng_ref[...] + lnb_ref[...]
    act = (yn * _sigmoid(yn)).astype(BF16)
    y = _dot(act, pw2_ref[...]) + pw2b_ref[...]
    x1 = x + rep(2) * y
    tb_ref[...] = _ffn(x1, nb, mod_ref, g_ref, up_ref, fdww_ref, fdwb_ref, down_ref, act_ref, carry_ref)
    _from_token_batch(tb_ref, pad_ref, out_ref)


def _convffn(x_tb, mod_t, norm_mix_g, pw1_w, pw1_b, dw_w, dw_b, ln_g, ln_b, pw2_w, pw2_b, norm_ffn_g, ffn_w):
    s, bn, d = x_tb.shape
    nt = TB_TOKENS
    rows = nt * bn
    row = lambda t: t.reshape(1, -1)
    return pl.pallas_call(
        _convffn_kernel,
        grid=(s // nt,),
        in_specs=[pl.BlockSpec((nt, bn, d), lambda i: (i, 0, 0)), _const_spec((6, bn, d)), _const_spec((1, d)),
                  _const_spec((d, 2 * d)), _const_spec((1, 2 * d)), _const_spec((CONV_WIDTH, d)),
                  _const_spec((1, d)), _const_spec((1, d)), _const_spec((1, d)),
                  _const_spec((d, d)), _const_spec((1, d)), _const_spec((1, d))] + _ffn_specs(),
        out_specs=pl.BlockSpec((bn, nt, d), lambda i: (0, i, 0)),
        out_shape=jax.ShapeDtypeStruct((bn, s, d), F32),
        scratch_shapes=[pltpu.VMEM((rows + (CONV_WIDTH - 1) * bn, d), F32), pltpu.VMEM((rows, d), F32),
                        _pad_scratch(bn, d), pltpu.VMEM((rows, d), F32)] + _ffn_scratch(rows, bn),
        compiler_params=_cparams(1),
        name="convffn",
    )(x_tb, mod_t, row(norm_mix_g), pw1_w.astype(BF16), row(pw1_b), dw_w, row(dw_b), row(ln_g), row(ln_b),
      pw2_w.astype(BF16), row(pw2_b), row(norm_ffn_g), *ffn_w)


def kernel(x, c, positions, ada_w, ada_b, norm_mix_g, norm_ffn_g, ab_w_in, a_vnorm_g, a_spatial_w, a_spatial_b, b_q_norm_g, b_k_norm_g, ab_w_out, conv_pw1_w, conv_pw1_b, conv_dw_w, conv_dw_b, conv_ln_g, conv_ln_b, conv_pw2_w, conv_pw2_b, ffn_up_w, ffn_dw_w, ffn_dw_b, ffn_down_w):
    bn, s, d = x.shape
    mod = _ada(c, ada_w, ada_b).reshape(ada_w.shape[0], bn, 6, d)
    cos, sin = _rope_tables(positions)

    ya, qkv = _inproj(x, mod[0], norm_mix_g[0], ab_w_in[0], a_vnorm_g[0], a_spatial_w[0],
                      a_spatial_b[0], b_q_norm_g[0], b_k_norm_g[0], cos, sin)
    branches = []
    for window, dilation in DILATED_PATTERNS:
        assert window // dilation == WINDOW_KEYS
        tq = min(512, s // dilation)
        branches += _attn_branch(*qkv[dilation], tq)
    x = _mix(x, mod[0], ya, branches, ab_w_out[0])
    mod_t = jnp.swapaxes(mod, 1, 2)
    ffn0 = _ffn_weights(ffn_up_w[0], ffn_dw_w[0], ffn_dw_b[0], ffn_down_w[0])
    x_tb = _ffn0(x, mod_t[0], norm_ffn_g[0], ffn0)

    ffn1 = _ffn_weights(ffn_up_w[1], ffn_dw_w[1], ffn_dw_b[1], ffn_down_w[1])
    return _convffn(x_tb, mod_t[1], norm_mix_g[1], conv_pw1_w[0], conv_pw1_b[0], conv_dw_w[0], conv_dw_b[0],
                    conv_ln_g[0], conv_ln_b[0], conv_pw2_w[0], conv_pw2_b[0], norm_ffn_g[1], ffn1)
```

```python
import math

import numpy as np
import jax
import jax.numpy as jnp
from jax import lax
from jax.experimental import pallas as pl
from jax.experimental.pallas import tpu as pltpu

F32 = jnp.float32
BF16 = jnp.bfloat16

D_MODEL = 1024
A_WIDTH = 512
A_GROUPS = 4
A_GROUP_DIM = 128
CHUNK = 128
B_WIDTH = 512
B_HEAD_DIM = 64
B_HEADS = 8
HEADS_PER_GROUP = 4
GROUP_LANES = HEADS_PER_GROUP * B_HEAD_DIM
N_GROUPS = B_HEADS // HEADS_PER_GROUP
DILATED_PATTERNS = ((128, 1), (512, 4), (2048, 16))
WINDOW_KEYS = 128
ROPE_THETA = 10000.0
CONV_WIDTH = 31
CONV_ROW_BLOCK = 64
GLU_CHUNK = 256
FFN_DIM = 2816
FFN_CONV_WIDTH = 3
FFN_CHUNK = 256
ROPE_ROW_BLOCK = 1024
ADA_COL_BLOCK = 1536
LSE_REP = 16
TB_TOKENS = 64
TB_PITCH = 72
EPS = 1e-6
NEG = -1e30

LANES = 128
SLABS = B_WIDTH // LANES
TOKEN_TILE = 512
V7X_VMEM_LIMIT_BYTES = 56 * 1024 * 1024


def _cparams(n_axes):
    return pltpu.CompilerParams(
        dimension_semantics=("arbitrary",) * n_axes,
        vmem_limit_bytes=V7X_VMEM_LIMIT_BYTES)


def _const_spec(shape):
    nd = len(shape)
    return pl.BlockSpec(shape, lambda *_: (0,) * nd, pipeline_mode=pl.Buffered(1))


def _dot(a, b):
    return jnp.dot(a, b, preferred_element_type=F32)


def _split_dot(a, b):
    hi = a.astype(BF16)
    lo = (a - hi.astype(F32)).astype(BF16)
    return _dot(hi, b) + _dot(lo, b)


def _sigmoid(x):
    return jax.nn.sigmoid(x)


def _gelu(x):
    return 0.5 * x * (1.0 + lax.erf(x * (1.0 / math.sqrt(2.0))))


def _modulate(x, g, shift, scale):
    ms = jnp.mean(x * x, axis=-1, keepdims=True)
    return (x * lax.rsqrt(ms + EPS) * g) * (1.0 + scale) + shift


def _ada_kernel(c_ref, w_ref, b_ref, o_ref):
    c = c_ref[...]
    act = (c * _sigmoid(c)).astype(BF16)
    o_ref[0] = _dot(act, w_ref[0].astype(BF16)) + b_ref[0]


def _ada(c, ada_w, ada_b):
    depth, d, n = ada_w.shape
    bn = c.shape[0]
    nb = ADA_COL_BLOCK
    return pl.pallas_call(
        _ada_kernel,
        grid=(depth, n // nb),
        in_specs=[
            pl.BlockSpec((bn, d), lambda l, j: (0, 0)),
            pl.BlockSpec((1, d, nb), lambda l, j: (l, 0, j)),
            pl.BlockSpec((1, 1, nb), lambda l, j: (l, 0, j)),
        ],
        out_specs=pl.BlockSpec((1, bn, nb), lambda l, j: (l, 0, j)),
        out_shape=jax.ShapeDtypeStruct((depth, bn, n), F32),
        compiler_params=_cparams(2),
        name="ada",
    )(c, ada_w, ada_b.reshape(depth, 1, n))


def _rope_kernel(pos_ref, inv_ref, cos_ref, sin_ref):
    rb = pos_ref.shape[0]
    half = B_HEAD_DIM // 2
    per_row = LANES // half
    ang = pos_ref[...].astype(F32) * inv_ref[...]
    cos, sin = jnp.cos(ang), jnp.sin(ang)
    src = lax.broadcasted_iota(jnp.int32, (LANES, LANES), 0)
    dst = lax.broadcasted_iota(jnp.int32, (LANES, LANES), 1)
    for j in range(per_row):
        spread = jnp.where((src // half == j) & (src % half == dst % half), 1.0, 0.0).astype(BF16)
        cos_ref[pl.ds(j, rb, stride=per_row), :] = _split_dot(cos, spread)
        sin_ref[pl.ds(j, rb, stride=per_row), :] = _split_dot(sin, spread)


def _rope_tables(positions):
    bn, s = positions.shape
    half = B_HEAD_DIM // 2
    inv_freq = 1.0 / (ROPE_THETA ** (jnp.arange(0, B_HEAD_DIM, 2, dtype=F32) / B_HEAD_DIM))
    per_row = LANES // half
    rows = bn * s // per_row
    pos_rep = jnp.repeat(positions.reshape(rows, per_row), half, axis=1)
    inv = jnp.tile(inv_freq, per_row).reshape(1, LANES)
    rb = min(rows, ROPE_ROW_BLOCK)
    cos, sin = pl.pallas_call(
        _rope_kernel,
        grid=(rows // rb,),
        in_specs=[pl.BlockSpec((rb, LANES), lambda i: (i, 0)),
                  pl.BlockSpec((1, LANES), lambda i: (0, 0))],
        out_specs=[pl.BlockSpec((per_row * rb, LANES), lambda i: (i, 0))] * 2,
        out_shape=[jax.ShapeDtypeStruct((bn * s, LANES), F32)] * 2,
        compiler_params=_cparams(1),
        name="rope",
    )(pos_rep, inv)
    return cos.reshape(bn, s, LANES), sin.reshape(bn, s, LANES)


def _qk_perm():
    half = B_HEAD_DIM // 2
    perm = np.zeros(B_WIDTH, np.int32)
    for grp in range(N_GROUPS):
        for n in range(GROUP_LANES):
            head = grp * HEADS_PER_GROUP + (n % LANES) // half
            perm[grp * GROUP_LANES + n] = head * B_HEAD_DIM + (n // LANES) * half + n % half
    return perm


def _deinterleave(tok_ref, r4_ref, out1_ref, out4_ref, out16_ref):
    tm = tok_ref.shape[1]
    q4, q16 = tm // 4, tm // 16
    for c in range(SLABS):
        lanes = slice(c * LANES, (c + 1) * LANES)
        out1_ref[0, :, lanes] = tok_ref[c].astype(BF16)
        for r in range(4):
            blk = tok_ref[c, pl.ds(r, q4, stride=4), :]
            out4_ref[0, r, :, lanes] = blk.astype(BF16)
            r4_ref[c, r * q4:(r + 1) * q4, :] = blk
        for r in range(4):
            for r2 in range(4):
                blk = r4_ref[c, pl.ds(r * q4 + r2, q16, stride=4), :]
                out16_ref[0, 4 * r2 + r, :, lanes] = blk.astype(BF16)


def _inproj_kernel(x_ref, mod_ref, g_ref, w_ref, vg_ref, ws_ref, wsb_ref, qg_ref, kg_ref,
                   cos_ref, sin_ref, ya_ref, q1_ref, k1_ref, v1_ref, q4_ref, k4_ref, v4_ref,
                   q16_ref, k16_ref, v16_ref, tok_ref, r4_ref):
    tm = x_ref.shape[1]
    x = x_ref[0]
    h = _modulate(x, g_ref[...], mod_ref[0, 0:1, :], mod_ref[0, 1:2, :]).astype(BF16)

    section = lambda k: _dot(h, w_ref[:, k * A_WIDTH:(k + 1) * A_WIDTH])
    z_ua, z_va = section(0), section(1)

    ua = _gelu(z_ua)
    z_q = section(2)
    va = _gelu(z_va)
    row = lax.broadcasted_iota(jnp.int32, (CHUNK, CHUNK), 0)
    col = lax.broadcasted_iota(jnp.int32, (CHUNK, CHUNK), 1)
    causal = row >= col
    vns = []
    for g in range(A_GROUPS):
        lanes = slice(g * A_GROUP_DIM, (g + 1) * A_GROUP_DIM)
        vg = va[:, lanes]
        mu = jnp.mean(vg, axis=-1, keepdims=True)
        var = jnp.mean(jnp.square(vg - mu), axis=-1, keepdims=True)
        vns.append(((vg - mu) * lax.rsqrt(var + EPS) * vg_ref[:, lanes]).astype(BF16))
    z_k = section(3)
    for g in range(A_GROUPS):
        lanes = slice(g * A_GROUP_DIM, (g + 1) * A_GROUP_DIM)
        ws = jnp.where(causal, ws_ref[g], 0.0).astype(BF16)
        bias = wsb_ref[:, g:g + 1]
        for c in range(tm // CHUNK):
            rows = slice(c * CHUNK, (c + 1) * CHUNK)
            f = _dot(ws, vns[g][rows, :]) + bias
            ya_ref[0, rows, lanes] = (ua[rows, lanes] * f).astype(BF16)
    z_v = section(4)

    seg_r = lax.broadcasted_iota(jnp.int32, (LANES, LANES), 0) // (B_HEAD_DIM // 2)
    seg_c = lax.broadcasted_iota(jnp.int32, (LANES, LANES), 1) // (B_HEAD_DIM // 2)
    seg_ones = jnp.where(seg_r == seg_c, 1.0, 0.0).astype(BF16)
    cos = cos_ref[0]
    sin = sin_ref[0]

    def norm_rot(z, gain_ref, tok, out_scale):
        for grp in range(N_GROUPS):
            lo = grp * GROUP_LANES
            z1 = z[:, lo:lo + LANES]
            z2 = z[:, lo + LANES:lo + 2 * LANES]
            ss = _split_dot(z1 * z1 + z2 * z2, seg_ones)
            inv = lax.rsqrt(ss * (1.0 / B_HEAD_DIM) + EPS)
            a1 = z1 * inv * gain_ref[:, lo:lo + LANES]
            a2 = z2 * inv * gain_ref[:, lo + LANES:lo + 2 * LANES]
            tok[2 * grp] = (a1 * cos - a2 * sin) * out_scale
            tok[2 * grp + 1] = (a2 * cos + a1 * sin) * out_scale

    norm_rot(z_q, qg_ref, tok_ref.at[0], B_HEAD_DIM ** -0.5)
    _deinterleave(tok_ref.at[0], r4_ref.at[0], q1_ref, q4_ref, q16_ref)
    norm_rot(z_k, kg_ref, tok_ref.at[1], 1.0)
    _deinterleave(tok_ref.at[1], r4_ref.at[1], k1_ref, k4_ref, k16_ref)
    for c in range(SLABS):
        tok_ref[2, c] = z_v[:, c * LANES:(c + 1) * LANES]
    _deinterleave(tok_ref.at[2], r4_ref.at[2], v1_ref, v4_ref, v16_ref)


def _inproj(x, mod, norm_g, w_in, a_vnorm_g, a_spatial_w, a_spatial_b, q_g, k_g, cos, sin):
    bn, s, d = x.shape
    tm = TOKEN_TILE
    perm = _qk_perm()
    qs = 2 * A_WIDTH
    w = jnp.concatenate([w_in[:, :qs], w_in[:, qs:qs + B_WIDTH][:, perm],
                         w_in[:, qs + B_WIDTH:qs + 2 * B_WIDTH][:, perm],
                         w_in[:, qs + 2 * B_WIDTH:]], axis=1).astype(BF16)
    half = B_HEAD_DIM // 2

    def gain_row(g):
        grp = jnp.concatenate([jnp.tile(g[:half], HEADS_PER_GROUP), jnp.tile(g[half:], HEADS_PER_GROUP)])
        return jnp.tile(grp, N_GROUPS).reshape(1, B_WIDTH)

    tok = lambda width: pl.BlockSpec((1, tm, width), lambda b, i: (b, i, 0))
    res = lambda dil: pl.BlockSpec((1, dil, tm // dil, B_WIDTH), lambda b, i: (b, 0, i, 0))
    sds1 = jax.ShapeDtypeStruct((bn, s, B_WIDTH), BF16)
    sds = lambda dil: jax.ShapeDtypeStruct((bn, dil, s // dil, B_WIDTH), BF16)
    outs = pl.pallas_call(
        _inproj_kernel,
        grid=(bn, s // tm),
        in_specs=[
            tok(d),
            pl.BlockSpec((1, 6, d), lambda b, i: (b, 0, 0)),
            _const_spec((1, d)),
            _const_spec(w.shape),
            _const_spec((1, A_WIDTH)),
            _const_spec((A_GROUPS, CHUNK, CHUNK)),
            _const_spec((CHUNK, A_GROUPS)),
            _const_spec((1, B_WIDTH)),
            _const_spec((1, B_WIDTH)),
            tok(LANES),
            tok(LANES),
        ],
        out_specs=[tok(A_WIDTH)] + [tok(B_WIDTH)] * 3 + [res(4)] * 3 + [res(16)] * 3,
        out_shape=[jax.ShapeDtypeStruct((bn, s, A_WIDTH), BF16)] + [sds1] * 3 + [sds(4)] * 3 + [sds(16)] * 3,
        scratch_shapes=[pltpu.VMEM((3, SLABS, tm, LANES), F32), pltpu.VMEM((3, SLABS, tm, LANES), F32)],
        compiler_params=_cparams(2),
        name="inproj",
    )(x, mod, norm_g.reshape(1, d), w, a_vnorm_g.reshape(1, A_WIDTH), a_spatial_w,
      a_spatial_b.T, gain_row(q_g), gain_row(k_g), cos, sin)
    ya = outs[0]
    qkv = {1: [t.reshape(bn, 1, s, B_WIDTH) for t in outs[1:4]], 4: outs[4:7], 16: outs[7:10]}
    return ya, qkv


def _attn_kernel(qmask_ref, vmask_ref, q_ref, kp_ref, kc_ref, vp_ref, vc_ref, o_ref, lse_ref):
    tq = q_ref.shape[0]
    nsub = tq // WINDOW_KEYS
    n = pl.program_id(2)
    qi = lax.broadcasted_iota(jnp.int32, (WINDOW_KEYS, 2 * WINDOW_KEYS), 0)
    kj = lax.broadcasted_iota(jnp.int32, (WINDOW_KEYS, 2 * WINDOW_KEYS), 1)
    dist = qi + WINDOW_KEYS - kj
    band = (dist >= 0) & (dist <= WINDOW_KEYS)
    first_mask = band & ((kj >= WINDOW_KEYS) | (n > 0))
    vlane = lax.broadcasted_iota(jnp.int32, (WINDOW_KEYS, GROUP_LANES), 1) // B_HEAD_DIM
    lse_pick = (lax.broadcasted_iota(jnp.int32, (WINDOW_KEYS, LANES), 1) % B_HEAD_DIM) // LSE_REP

    def window(cur_ref, prev_ref, i):
        rows = slice(i * WINDOW_KEYS, (i + 1) * WINDOW_KEYS)
        prev = prev_ref[...] if i == 0 else cur_ref[(i - 1) * WINDOW_KEYS:i * WINDOW_KEYS, :]
        return jnp.concatenate([prev, cur_ref[rows, :]], axis=0)

    def scores(i):
        rows = slice(i * WINDOW_KEYS, (i + 1) * WINDOW_KEYS)
        kk = window(kc_ref, kp_ref, i)
        q = q_ref[rows, :]
        mask = first_mask if i == 0 else band
        out = []
        for grp in range(N_GROUPS):
            lanes = slice(grp * GROUP_LANES, (grp + 1) * GROUP_LANES)
            qg, kg = q[:, lanes], kk[:, lanes]
            for hh in range(HEADS_PER_GROUP):
                qm = jnp.where(qmask_ref[hh] > 0, qg, jnp.zeros_like(qg))
                sc = lax.dot_general(qm, kg, (((1,), (1,)), ((), ())), preferred_element_type=F32)
                out.append(jnp.where(mask, sc, NEG))
        return out

    def finish(i, scs):
        rows = slice(i * WINDOW_KEYS, (i + 1) * WINDOW_KEYS)
        vv = window(vc_ref, vp_ref, i)
        head_sel = vmask_ref[...]
        lse_parts = []
        for grp in range(N_GROUPS):
            lanes = slice(grp * GROUP_LANES, (grp + 1) * GROUP_LANES)
            vg = vv[:, lanes]
            v_bd = jnp.concatenate([vg] * HEADS_PER_GROUP, axis=0)
            v_bd = jnp.where(head_sel > 0, v_bd, jnp.zeros_like(v_bd))
            ps, ms = [], []
            for hh in range(HEADS_PER_GROUP):
                sc = scs[grp * HEADS_PER_GROUP + hh]
                m = jnp.max(sc, axis=-1, keepdims=True)
                ps.append(jnp.exp(sc - m).astype(BF16))
                ms.append(m)
            p_cat = jnp.concatenate(ps, axis=1)
            pv = _dot(p_cat, v_bd)
            den = _dot(p_cat, head_sel)
            m_cat = jnp.broadcast_to(ms[-1], (WINDOW_KEYS, GROUP_LANES))
            for hh in range(HEADS_PER_GROUP - 1):
                m_cat = jnp.where(vlane == hh, ms[hh], m_cat)
            o_ref[rows, lanes] = (pv / den).astype(BF16)
            lse = m_cat + jnp.log(den)
            lse_parts += [lse[:, :LANES], lse[:, LANES:]]
        tile = lse_parts[-1]
        for part in range(len(lse_parts) - 1):
            tile = jnp.where(lse_pick == part, lse_parts[part], tile)
        lse_ref[rows, :] = tile

    pending = scores(0)
    for i in range(nsub):
        nxt = scores(i + 1) if i + 1 < nsub else None
        finish(i, pending)
        pending = nxt


def _attn_branch(q, k, v, tq):
    bn, dilation, sub_len, w = q.shape
    blocks_per_step = tq // WINDOW_KEYS
    cur = pl.BlockSpec((None, None, tq, w), lambda b, r, n: (b, r, n, 0))
    prev = pl.BlockSpec((None, None, WINDOW_KEYS, w),
                        lambda b, r, n: (b, r, jnp.maximum(n * blocks_per_step - 1, 0), 0))
    half = B_HEAD_DIM // 2
    lane = np.arange(GROUP_LANES)
    qmask = np.stack([np.broadcast_to(((lane % LANES) // half == hh), (WINDOW_KEYS, GROUP_LANES))
                      for hh in range(HEADS_PER_GROUP)]).astype(np.float32)
    vmask = np.concatenate([np.broadcast_to((lane // B_HEAD_DIM == hh), (2 * WINDOW_KEYS, GROUP_LANES))
                            for hh in range(HEADS_PER_GROUP)]).astype(np.float32)
    return pl.pallas_call(
        _attn_kernel,
        grid=(bn, dilation, sub_len // tq),
        in_specs=[_const_spec(qmask.shape), _const_spec(vmask.shape), cur, prev, cur, prev, cur],
        out_specs=[cur, pl.BlockSpec((None, None, tq, LANES), lambda b, r, n: (b, r, n, 0))],
        out_shape=[jax.ShapeDtypeStruct(q.shape, BF16),
                   jax.ShapeDtypeStruct((bn, dilation, sub_len, LANES), F32)],
        compiler_params=_cparams(3),
        name=f"attn_d{dilation}",
    )(jnp.asarray(qmask, BF16), jnp.asarray(vmask, BF16), q, k, k, v, v)


def _to_token_batch(x_ref, pad_ref, tb_ref):
    nb, nt, d = x_ref.shape
    for c in range(d // LANES):
        lanes = slice(c * LANES, (c + 1) * LANES)
        for b in range(nb):
            pad_ref[c, b * TB_PITCH:b * TB_PITCH + nt, :] = x_ref[b, :, lanes]
        for t in range(nt):
            tb_ref[t * nb:(t + 1) * nb, lanes] = pad_ref[c, pl.ds(t, nb, stride=TB_PITCH), :]


def _from_token_batch(tb_ref, pad_ref, out_ref):
    nb, nt, d = out_ref.shape
    for c in range(d // LANES):
        lanes = slice(c * LANES, (c + 1) * LANES)
        for t in range(nt):
            pad_ref[c, pl.ds(t, nb, stride=TB_PITCH), :] = tb_ref[t * nb:(t + 1) * nb, lanes]
        for b in range(nb):
            out_ref[b, :, lanes] = pad_ref[c, b * TB_PITCH:b * TB_PITCH + nt, :]


def _ffn(x1, nb, mod_ref, g_ref, up_ref, dww_ref, dwb_ref, down_ref, act_ref, carry_ref):
    rows = x1.shape[0]
    rep = lambda k: jnp.tile(mod_ref[k], (rows // nb, 1))
    h = _modulate(x1, g_ref[...], rep(3), rep(4)).astype(BF16)

    def up(c):
        return (_dot(h, up_ref[:, c * FFN_CHUNK:(c + 1) * FFN_CHUNK]),
                _dot(h, up_ref[:, FFN_DIM + c * FFN_CHUNK:FFN_DIM + (c + 1) * FFN_CHUNK]))

    def conv(z, col0):
        cols = slice(col0, col0 + FFN_CHUNK)
        hist = carry_ref[:, cols]
        carry_ref[:, cols] = z[rows - 2 * nb:, :]
        z1 = jnp.concatenate([hist[nb:], z[:rows - nb]], axis=0)
        z2 = jnp.concatenate([hist, z[:rows - 2 * nb]], axis=0)
        return (z * dww_ref[2:3, cols] + z1 * dww_ref[1:2, cols] + z2 * dww_ref[0:1, cols]
                + dwb_ref[:, cols])

    nch = FFN_DIM // FFN_CHUNK
    nxt = up(0)
    for c in range(nch):
        za, zb = nxt
        if c + 1 < nch:
            nxt = up(c + 1)
        a = conv(za, c * FFN_CHUNK)
        b = conv(zb, FFN_DIM + c * FFN_CHUNK)
        act_ref[:, c * FFN_CHUNK:(c + 1) * FFN_CHUNK] = (a * _sigmoid(a) * b).astype(BF16)
    return x1 + rep(5) * _dot(act_ref[...], down_ref[...])


def _ffn_weights(up_w, dw_w, dw_b, down_w):
    return up_w.astype(BF16), dw_w, dw_b.reshape(-1, 1, 2 * FFN_DIM), down_w.astype(BF16)


def _ffn_specs(layer):
    pick = lambda *shape: pl.BlockSpec((None,) + shape, lambda *_: (layer, 0, 0), pipeline_mode=pl.Buffered(1))
    return [pick(D_MODEL, 2 * FFN_DIM), pick(FFN_CONV_WIDTH, 2 * FFN_DIM), pick(1, 2 * FFN_DIM),
            pick(FFN_DIM, D_MODEL)]


def _ffn_scratch(rows, nb):
    return [pltpu.VMEM((rows, FFN_DIM), BF16), pltpu.VMEM(((FFN_CONV_WIDTH - 1) * nb, 2 * FFN_DIM), F32)]


def _pad_scratch(nb, d):
    return pltpu.VMEM((d // LANES, nb * TB_PITCH, LANES), F32)


def _ffn0_kernel(x_ref, mod_ref, g_ref, up_ref, dww_ref, dwb_ref, down_ref, out_ref,
                 pad_ref, tb_ref, act_ref, carry_ref):
    nb = x_ref.shape[0]

    @pl.when(pl.program_id(0) == 0)
    def _():
        carry_ref[...] = jnp.zeros_like(carry_ref)

    _to_token_batch(x_ref, pad_ref, tb_ref)
    y = _ffn(tb_ref[...], nb, mod_ref, g_ref, up_ref, dww_ref, dwb_ref, down_ref, act_ref, carry_ref)
    out_ref[...] = y.reshape(out_ref.shape)


def _ffn0(x, mod_t, norm_g, ffn_w):
    bn, s, d = x.shape
    nt = TB_TOKENS
    rows = nt * bn
    return pl.pallas_call(
        _ffn0_kernel,
        grid=(s // nt,),
        in_specs=[pl.BlockSpec((bn, nt, d), lambda i: (0, i, 0)), _const_spec((6, bn, d)),
                  _const_spec((1, d))] + _ffn_specs(0),
        out_specs=pl.BlockSpec((nt, bn, d), lambda i: (i, 0, 0)),
        out_shape=jax.ShapeDtypeStruct((s, bn, d), F32),
        scratch_shapes=[_pad_scratch(bn, d), pltpu.VMEM((rows, d), F32)] + _ffn_scratch(rows, bn),
        compiler_params=_cparams(1),
        name="ffn0",
    )(x, mod_t, norm_g.reshape(1, d), *ffn_w)


def _interleave4(src_ref, dst_ref):
    q4 = src_ref.shape[1]
    for c in range(src_ref.shape[2] // LANES):
        for r in range(4):
            dst_ref[c, pl.ds(r, q4, stride=4), :] = src_ref[r, :, c * LANES:(c + 1) * LANES].astype(F32)


def _interleave16(src_ref, mid_ref, dst_ref):
    q16 = src_ref.shape[1]
    q4 = 4 * q16
    for c in range(src_ref.shape[2] // LANES):
        for r in range(4):
            for r2 in range(4):
                mid_ref[c, pl.ds(r * q4 + r2, q16, stride=4), :] = (
                    src_ref[4 * r2 + r, :, c * LANES:(c + 1) * LANES].astype(F32))
        for r in range(4):
            dst_ref[c, pl.ds(r, q4, stride=4), :] = mid_ref[c, r * q4:(r + 1) * q4, :]


def _slabs(ref):
    return jnp.concatenate([ref[c] for c in range(ref.shape[0])], axis=1)


def _lse_lane(head):
    return B_HEAD_DIM * (head % 2) + LSE_REP * (head // 2)


def _mix_kernel(x_ref, mod_ref, ya_ref, o1_ref, l1_ref, o4_ref, l4_ref, o16_ref, l16_ref,
                exp_ref, wout_ref, out_ref, ilo_ref, mido_ref, ill_ref, midl_ref):
    _interleave4(o4_ref, ilo_ref.at[0])
    _interleave16(o16_ref, mido_ref, ilo_ref.at[1])
    _interleave4(l4_ref, ill_ref.at[0])
    _interleave16(l16_ref, midl_ref, ill_ref.at[1])
    l1, l2, l3 = l1_ref[...], ill_ref[0, 0], ill_ref[1, 0]
    mx = jnp.maximum(l1, jnp.maximum(l2, l3))
    e1, e2, e3 = jnp.exp(l1 - mx), jnp.exp(l2 - mx), jnp.exp(l3 - mx)
    tot = e1 + e2 + e3
    expand = exp_ref[...]
    yb = (_split_dot(e1 / tot, expand) * o1_ref[...].astype(F32)
          + _split_dot(e2 / tot, expand) * _slabs(ilo_ref.at[0])
          + _split_dot(e3 / tot, expand) * _slabs(ilo_ref.at[1]))
    y = _dot(ya_ref[0], wout_ref[0:A_WIDTH, :]) + _dot(yb.astype(BF16), wout_ref[A_WIDTH:, :])
    out_ref[0] = x_ref[0] + mod_ref[0, 2:3, :] * y


def _mix(x, mod, ya, branches, w_out):
    bn, s, d = x.shape
    tm = TOKEN_TILE
    tok = lambda width: pl.BlockSpec((1, tm, width), lambda b, i: (b, i, 0))
    branch_specs = []
    for dil in (1, 4, 16):
        for width in (B_WIDTH, LANES):
            if dil == 1:
                branch_specs.append(pl.BlockSpec((None, None, tm, width), lambda b, i: (b, 0, i, 0)))
            else:
                branch_specs.append(pl.BlockSpec((None, dil, tm // dil, width), lambda b, i: (b, 0, i, 0)))
    expand = np.zeros((LANES, B_WIDTH), np.float32)
    for head in range(B_HEADS):
        expand[_lse_lane(head), head * B_HEAD_DIM:(head + 1) * B_HEAD_DIM] = 1.0
    return pl.pallas_call(
        _mix_kernel,
        grid=(bn, s // tm),
        in_specs=[tok(d), pl.BlockSpec((1, 6, d), lambda b, i: (b, 0, 0)), tok(A_WIDTH)] + branch_specs
                 + [_const_spec((LANES, B_WIDTH)), _const_spec((A_WIDTH + B_WIDTH, d))],
        out_specs=tok(d),
        out_shape=jax.ShapeDtypeStruct((bn, s, d), F32),
        scratch_shapes=[pltpu.VMEM((2, SLABS, tm, LANES), F32), pltpu.VMEM((SLABS, tm, LANES), F32),
                        pltpu.VMEM((2, 1, tm, LANES), F32), pltpu.VMEM((1, tm, LANES), F32)],
        compiler_params=_cparams(2),
        name="mix",
    )(x, mod, ya, *branches, jnp.asarray(expand, BF16), w_out.astype(BF16))


def _convffn_kernel(x_ref, mod_ref, gm_ref, pw1_ref, pw1b_ref, dw_ref, dwb_ref, lng_ref, lnb_ref,
                    pw2_ref, pw2b_ref, g_ref, up_ref, fdww_ref, fdwb_ref, down_ref, out_ref,
                    ybuf_ref, cv_ref, pad_ref, tb_ref, act_ref, carry_ref):
    nt, nb, d = x_ref.shape
    rows = nt * nb
    hist = (CONV_WIDTH - 1) * nb

    @pl.when(pl.program_id(0) == 0)
    def _():
        carry_ref[...] = jnp.zeros_like(carry_ref)
        ybuf_ref[0:hist, :] = jnp.zeros((hist, d), F32)

    rep = lambda k: jnp.tile(mod_ref[k], (nt, 1))
    x = x_ref[...].reshape(rows, d)
    h = _modulate(x, gm_ref[...], rep(0), rep(1)).astype(BF16)

    def pw1(j):
        cols = slice(j * GLU_CHUNK, (j + 1) * GLU_CHUNK)
        gcols = slice(d + j * GLU_CHUNK, d + (j + 1) * GLU_CHUNK)
        return (_dot(h, pw1_ref[:, cols]) + pw1b_ref[:, cols],
                _dot(h, pw1_ref[:, gcols]) + pw1b_ref[:, gcols])

    rb = CONV_ROW_BLOCK
    n_glu = d // GLU_CHUNK
    nxt = pw1(0)
    for j in range(n_glu):
        a, gate = nxt
        if j + 1 < n_glu:
            nxt = pw1(j + 1)
        lanes = slice(j * GLU_CHUNK, (j + 1) * GLU_CHUNK)
        ybuf_ref[hist:hist + rows, lanes] = a * _sigmoid(gate)
        for r in range(rows // rb):
            acc = jnp.zeros((rb, GLU_CHUNK), F32) + dwb_ref[:, lanes]
            for k in range(CONV_WIDTH):
                start = r * rb + k * nb
                acc = acc + ybuf_ref[start:start + rb, lanes] * dw_ref[k:k + 1, lanes]
            cv_ref[r * rb:(r + 1) * rb, lanes] = acc
        ybuf_ref[0:hist, lanes] = ybuf_ref[rows:rows + hist, lanes]

    cv = cv_ref[...]
    mu = jnp.mean(cv, axis=-1, keepdims=True)
    var = jnp.mean(jnp.square(cv - mu), axis=-1, keepdims=True)
    yn = (cv - mu) * lax.rsqrt(var + EPS) * lng_ref[...] + lnb_ref[...]
    act = (yn * _sigmoid(yn)).astype(BF16)
    y = _dot(act, pw2_ref[...]) + pw2b_ref[...]
    x1 = x + rep(2) * y
    tb_ref[...] = _ffn(x1, nb, mod_ref, g_ref, up_ref, fdww_ref, fdwb_ref, down_ref, act_ref, carry_ref)
    _from_token_batch(tb_ref, pad_ref, out_ref)


def _convffn(x_tb, mod_t, norm_mix_g, pw1_w, pw1_b, dw_w, dw_b, ln_g, ln_b, pw2_w, pw2_b, norm_ffn_g, ffn_w):
    s, bn, d = x_tb.shape
    nt = TB_TOKENS
    rows = nt * bn
    row = lambda t: t.reshape(1, -1)
    return pl.pallas_call(
        _convffn_kernel,
        grid=(s // nt,),
        in_specs=[pl.BlockSpec((nt, bn, d), lambda i: (i, 0, 0)), _const_spec((6, bn, d)), _const_spec((1, d)),
                  _const_spec((d, 2 * d)), _const_spec((1, 2 * d)), _const_spec((CONV_WIDTH, d)),
                  _const_spec((1, d)), _const_spec((1, d)), _const_spec((1, d)),
                  _const_spec((d, d)), _const_spec((1, d)), _const_spec((1, d))] + _ffn_specs(1),
        out_specs=pl.BlockSpec((bn, nt, d), lambda i: (0, i, 0)),
        out_shape=jax.ShapeDtypeStruct((bn, s, d), F32),
        scratch_shapes=[pltpu.VMEM((rows + (CONV_WIDTH - 1) * bn, d), F32), pltpu.VMEM((rows, d), F32),
                        _pad_scratch(bn, d), pltpu.VMEM((rows, d), F32)] + _ffn_scratch(rows, bn),
        compiler_params=_cparams(1),
        name="convffn",
    )(x_tb, mod_t, row(norm_mix_g), pw1_w.astype(BF16), row(pw1_b), dw_w, row(dw_b), row(ln_g), row(ln_b),
      pw2_w.astype(BF16), row(pw2_b), row(norm_ffn_g), *ffn_w)


def kernel(x, c, positions, ada_w, ada_b, norm_mix_g, norm_ffn_g, ab_w_in, a_vnorm_g, a_spatial_w, a_spatial_b, b_q_norm_g, b_k_norm_g, ab_w_out, conv_pw1_w, conv_pw1_b, conv_dw_w, conv_dw_b, conv_ln_g, conv_ln_b, conv_pw2_w, conv_pw2_b, ffn_up_w, ffn_dw_w, ffn_dw_b, ffn_down_w):
    bn, s, d = x.shape
    mod = _ada(c, ada_w, ada_b).reshape(ada_w.shape[0], bn, 6, d)
    cos, sin = _rope_tables(positions)

    ya, qkv = _inproj(x, mod[0], norm_mix_g[0], ab_w_in[0], a_vnorm_g[0], a_spatial_w[0],
                      a_spatial_b[0], b_q_norm_g[0], b_k_norm_g[0], cos, sin)
    branches = []
    for window, dilation in DILATED_PATTERNS:
        assert window // dilation == WINDOW_KEYS
        tq = min(512, s // dilation)
        branches += _attn_branch(*qkv[dilation], tq)
    x = _mix(x, mod[0], ya, branches, ab_w_out[0])
    mod_t = jnp.swapaxes(mod, 1, 2)
    ffn_w = _ffn_weights(ffn_up_w, ffn_dw_w, ffn_dw_b, ffn_down_w)
    x_tb = _ffn0(x, mod_t[0], norm_ffn_g[0], ffn_w)

    return _convffn(x_tb, mod_t[1], norm_mix_g[1], conv_pw1_w[0], conv_pw1_b[0], conv_dw_w[0], conv_dw_b[0],
                    conv_ln_g[0], conv_ln_b[0], conv_pw2_w[0], conv_pw2_b[0], norm_ffn_g[1], ffn_w)
```

```python
import math

import numpy as np
import jax
import jax.numpy as jnp
from jax import lax
from jax.experimental import pallas as pl
from jax.experimental.pallas import tpu as pltpu

F32 = jnp.float32
BF16 = jnp.bfloat16

D_MODEL = 1024
A_WIDTH = 512
A_GROUPS = 4
A_GROUP_DIM = 128
CHUNK = 128
B_WIDTH = 512
B_HEAD_DIM = 64
B_HEADS = 8
HEADS_PER_GROUP = 4
GROUP_LANES = HEADS_PER_GROUP * B_HEAD_DIM
N_GROUPS = B_HEADS // HEADS_PER_GROUP
DILATED_PATTERNS = ((128, 1), (512, 4), (2048, 16))
WINDOW_KEYS = 128
ATTN_QUERY_TILE = 1024
ROPE_THETA = 10000.0
CONV_WIDTH = 31
CONV_ROW_BLOCK = 64
GLU_CHUNK = 256
FFN_DIM = 2816
FFN_CONV_WIDTH = 3
FFN_CHUNK = 256
ROPE_ROW_BLOCK = 1024
ADA_COL_BLOCK = 1536
LSE_REP = 16
TB_TOKENS = 64
TB_PITCH = 72
EPS = 1e-6
NEG = -1e30

LANES = 128
SLABS = B_WIDTH // LANES
TOKEN_TILE = 512
V7X_VMEM_LIMIT_BYTES = 56 * 1024 * 1024


def _cparams(n_axes):
    return pltpu.CompilerParams(
        dimension_semantics=("arbitrary",) * n_axes,
        vmem_limit_bytes=V7X_VMEM_LIMIT_BYTES)


def _const_spec(shape):
    nd = len(shape)
    return pl.BlockSpec(shape, lambda *_: (0,) * nd, pipeline_mode=pl.Buffered(1))


def _dot(a, b):
    return jnp.dot(a, b, preferred_element_type=F32)


def _split_dot(a, b):
    hi = a.astype(BF16)
    lo = (a - hi.astype(F32)).astype(BF16)
    return _dot(hi, b) + _dot(lo, b)


def _sigmoid(x):
    return jax.nn.sigmoid(x)


def _gelu(x):
    return 0.5 * x * (1.0 + lax.erf(x * (1.0 / math.sqrt(2.0))))


def _modulate(x, g, shift, scale):
    ms = jnp.mean(x * x, axis=-1, keepdims=True)
    return (x * lax.rsqrt(ms + EPS) * g) * (1.0 + scale) + shift


def _ada_kernel(c_ref, w_ref, b_ref, o_ref):
    c = c_ref[...]
    act = (c * _sigmoid(c)).astype(BF16)
    o_ref[0] = _dot(act, w_ref[0].astype(BF16)) + b_ref[0]


def _ada(c, ada_w, ada_b):
    depth, d, n = ada_w.shape
    bn = c.shape[0]
    nb = ADA_COL_BLOCK
    return pl.pallas_call(
        _ada_kernel,
        grid=(depth, n // nb),
        in_specs=[
            pl.BlockSpec((bn, d), lambda l, j: (0, 0)),
            pl.BlockSpec((1, d, nb), lambda l, j: (l, 0, j)),
            pl.BlockSpec((1, 1, nb), lambda l, j: (l, 0, j)),
        ],
        out_specs=pl.BlockSpec((1, bn, nb), lambda l, j: (l, 0, j)),
        out_shape=jax.ShapeDtypeStruct((depth, bn, n), F32),
        compiler_params=_cparams(2),
        name="ada",
    )(c, ada_w, ada_b.reshape(depth, 1, n))


def _rope_kernel(pos_ref, inv_ref, cos_ref, sin_ref):
    rb = pos_ref.shape[0]
    half = B_HEAD_DIM // 2
    per_row = LANES // half
    ang = pos_ref[...].astype(F32) * inv_ref[...]
    cos, sin = jnp.cos(ang), jnp.sin(ang)
    src = lax.broadcasted_iota(jnp.int32, (LANES, LANES), 0)
    dst = lax.broadcasted_iota(jnp.int32, (LANES, LANES), 1)
    for j in range(per_row):
        spread = jnp.where((src // half == j) & (src % half == dst % half), 1.0, 0.0).astype(BF16)
        cos_ref[pl.ds(j, rb, stride=per_row), :] = _split_dot(cos, spread)
        sin_ref[pl.ds(j, rb, stride=per_row), :] = _split_dot(sin, spread)


def _rope_tables(positions):
    bn, s = positions.shape
    half = B_HEAD_DIM // 2
    inv_freq = 1.0 / (ROPE_THETA ** (jnp.arange(0, B_HEAD_DIM, 2, dtype=F32) / B_HEAD_DIM))
    per_row = LANES // half
    rows = bn * s // per_row
    pos_rep = jnp.repeat(positions.reshape(rows, per_row), half, axis=1)
    inv = jnp.tile(inv_freq, per_row).reshape(1, LANES)
    rb = min(rows, ROPE_ROW_BLOCK)
    cos, sin = pl.pallas_call(
        _rope_kernel,
        grid=(rows // rb,),
        in_specs=[pl.BlockSpec((rb, LANES), lambda i: (i, 0)),
                  pl.BlockSpec((1, LANES), lambda i: (0, 0))],
        out_specs=[pl.BlockSpec((per_row * rb, LANES), lambda i: (i, 0))] * 2,
        out_shape=[jax.ShapeDtypeStruct((bn * s, LANES), F32)] * 2,
        compiler_params=_cparams(1),
        name="rope",
    )(pos_rep, inv)
    return cos.reshape(bn, s, LANES), sin.reshape(bn, s, LANES)


def _qk_perm():
    half = B_HEAD_DIM // 2
    perm = np.zeros(B_WIDTH, np.int32)
    for grp in range(N_GROUPS):
        for n in range(GROUP_LANES):
            head = grp * HEADS_PER_GROUP + (n % LANES) // half
            perm[grp * GROUP_LANES + n] = head * B_HEAD_DIM + (n // LANES) * half + n % half
    return perm


def _deinterleave(tok_ref, r4_ref, out1_ref, out4_ref, out16_ref):
    tm = tok_ref.shape[1]
    q4, q16 = tm // 4, tm // 16
    for c in range(SLABS):
        lanes = slice(c * LANES, (c + 1) * LANES)
        out1_ref[0, :, lanes] = tok_ref[c].astype(BF16)
        for r in range(4):
            blk = tok_ref[c, pl.ds(r, q4, stride=4), :]
            out4_ref[0, r, :, lanes] = blk.astype(BF16)
            r4_ref[c, r * q4:(r + 1) * q4, :] = blk
        for r in range(4):
            for r2 in range(4):
                blk = r4_ref[c, pl.ds(r * q4 + r2, q16, stride=4), :]
                out16_ref[0, 4 * r2 + r, :, lanes] = blk.astype(BF16)


def _inproj_kernel(x_ref, mod_ref, g_ref, w_ref, vg_ref, ws_ref, wsb_ref, qg_ref, kg_ref,
                   cos_ref, sin_ref, ya_ref, q1_ref, k1_ref, v1_ref, q4_ref, k4_ref, v4_ref,
                   q16_ref, k16_ref, v16_ref, tok_ref, r4_ref):
    tm = x_ref.shape[1]
    x = x_ref[0]
    h = _modulate(x, g_ref[...], mod_ref[0, 0:1, :], mod_ref[0, 1:2, :]).astype(BF16)

    section = lambda k: _dot(h, w_ref[:, k * A_WIDTH:(k + 1) * A_WIDTH])
    z_ua, z_va = section(0), section(1)

    ua = _gelu(z_ua)
    z_q = section(2)
    va = _gelu(z_va)
    row = lax.broadcasted_iota(jnp.int32, (CHUNK, CHUNK), 0)
    col = lax.broadcasted_iota(jnp.int32, (CHUNK, CHUNK), 1)
    causal = row >= col
    vns = []
    for g in range(A_GROUPS):
        lanes = slice(g * A_GROUP_DIM, (g + 1) * A_GROUP_DIM)
        vg = va[:, lanes]
        mu = jnp.mean(vg, axis=-1, keepdims=True)
        var = jnp.mean(jnp.square(vg - mu), axis=-1, keepdims=True)
        vns.append(((vg - mu) * lax.rsqrt(var + EPS) * vg_ref[:, lanes]).astype(BF16))
    z_k = section(3)
    for g in range(A_GROUPS):
        lanes = slice(g * A_GROUP_DIM, (g + 1) * A_GROUP_DIM)
        ws = jnp.where(causal, ws_ref[g], 0.0).astype(BF16)
        bias = wsb_ref[:, g:g + 1]
        for c in range(tm // CHUNK):
            rows = slice(c * CHUNK, (c + 1) * CHUNK)
            f = _dot(ws, vns[g][rows, :]) + bias
            ya_ref[0, rows, lanes] = (ua[rows, lanes] * f).astype(BF16)
    z_v = section(4)

    seg_r = lax.broadcasted_iota(jnp.int32, (LANES, LANES), 0) // (B_HEAD_DIM // 2)
    seg_c = lax.broadcasted_iota(jnp.int32, (LANES, LANES), 1) // (B_HEAD_DIM // 2)
    seg_ones = jnp.where(seg_r == seg_c, 1.0, 0.0).astype(BF16)
    cos = cos_ref[0]
    sin = sin_ref[0]

    def norm_rot(z, gain_ref, tok, out_scale):
        for grp in range(N_GROUPS):
            lo = grp * GROUP_LANES
            z1 = z[:, lo:lo + LANES]
            z2 = z[:, lo + LANES:lo + 2 * LANES]
            ss = _split_dot(z1 * z1 + z2 * z2, seg_ones)
            inv = lax.rsqrt(ss * (1.0 / B_HEAD_DIM) + EPS)
            a1 = z1 * inv * gain_ref[:, lo:lo + LANES]
            a2 = z2 * inv * gain_ref[:, lo + LANES:lo + 2 * LANES]
            tok[2 * grp] = (a1 * cos - a2 * sin) * out_scale
            tok[2 * grp + 1] = (a2 * cos + a1 * sin) * out_scale

    norm_rot(z_q, qg_ref, tok_ref.at[0], B_HEAD_DIM ** -0.5)
    _deinterleave(tok_ref.at[0], r4_ref.at[0], q1_ref, q4_ref, q16_ref)
    norm_rot(z_k, kg_ref, tok_ref.at[1], 1.0)
    _deinterleave(tok_ref.at[1], r4_ref.at[1], k1_ref, k4_ref, k16_ref)
    for c in range(SLABS):
        tok_ref[2, c] = z_v[:, c * LANES:(c + 1) * LANES]
    _deinterleave(tok_ref.at[2], r4_ref.at[2], v1_ref, v4_ref, v16_ref)


def _inproj(x, mod, norm_g, w_in, a_vnorm_g, a_spatial_w, a_spatial_b, q_g, k_g, cos, sin):
    bn, s, d = x.shape
    tm = TOKEN_TILE
    perm = _qk_perm()
    qs = 2 * A_WIDTH
    w = jnp.concatenate([w_in[:, :qs], w_in[:, qs:qs + B_WIDTH][:, perm],
                         w_in[:, qs + B_WIDTH:qs + 2 * B_WIDTH][:, perm],
                         w_in[:, qs + 2 * B_WIDTH:]], axis=1).astype(BF16)
    half = B_HEAD_DIM // 2

    def gain_row(g):
        grp = jnp.concatenate([jnp.tile(g[:half], HEADS_PER_GROUP), jnp.tile(g[half:], HEADS_PER_GROUP)])
        return jnp.tile(grp, N_GROUPS).reshape(1, B_WIDTH)

    tok = lambda width: pl.BlockSpec((1, tm, width), lambda b, i: (b, i, 0))
    res = lambda dil: pl.BlockSpec((1, dil, tm // dil, B_WIDTH), lambda b, i: (b, 0, i, 0))
    sds1 = jax.ShapeDtypeStruct((bn, s, B_WIDTH), BF16)
    sds = lambda dil: jax.ShapeDtypeStruct((bn, dil, s // dil, B_WIDTH), BF16)
    outs = pl.pallas_call(
        _inproj_kernel,
        grid=(bn, s // tm),
        in_specs=[
            tok(d),
            pl.BlockSpec((1, 6, d), lambda b, i: (b, 0, 0)),
            _const_spec((1, d)),
            _const_spec(w.shape),
            _const_spec((1, A_WIDTH)),
            _const_spec((A_GROUPS, CHUNK, CHUNK)),
            _const_spec((CHUNK, A_GROUPS)),
            _const_spec((1, B_WIDTH)),
            _const_spec((1, B_WIDTH)),
            tok(LANES),
            tok(LANES),
        ],
        out_specs=[tok(A_WIDTH)] + [tok(B_WIDTH)] * 3 + [res(4)] * 3 + [res(16)] * 3,
        out_shape=[jax.ShapeDtypeStruct((bn, s, A_WIDTH), BF16)] + [sds1] * 3 + [sds(4)] * 3 + [sds(16)] * 3,
        scratch_shapes=[pltpu.VMEM((3, SLABS, tm, LANES), F32), pltpu.VMEM((3, SLABS, tm, LANES), F32)],
        compiler_params=_cparams(2),
        name="inproj",
    )(x, mod, norm_g.reshape(1, d), w, a_vnorm_g.reshape(1, A_WIDTH), a_spatial_w,
      a_spatial_b.T, gain_row(q_g), gain_row(k_g), cos, sin)
    ya = outs[0]
    qkv = {1: [t.reshape(bn, 1, s, B_WIDTH) for t in outs[1:4]], 4: outs[4:7], 16: outs[7:10]}
    return ya, qkv


def _attn_kernel(qmask_ref, vmask_ref, q_ref, kp_ref, kc_ref, vp_ref, vc_ref, o_ref, lse_ref):
    tq = q_ref.shape[0]
    nsub = tq // WINDOW_KEYS
    n = pl.program_id(2)
    qi = lax.broadcasted_iota(jnp.int32, (WINDOW_KEYS, 2 * WINDOW_KEYS), 0)
    kj = lax.broadcasted_iota(jnp.int32, (WINDOW_KEYS, 2 * WINDOW_KEYS), 1)
    dist = qi + WINDOW_KEYS - kj
    band = (dist >= 0) & (dist <= WINDOW_KEYS)
    first_mask = band & ((kj >= WINDOW_KEYS) | (n > 0))
    vlane = lax.broadcasted_iota(jnp.int32, (WINDOW_KEYS, GROUP_LANES), 1) // B_HEAD_DIM
    lse_pick = (lax.broadcasted_iota(jnp.int32, (WINDOW_KEYS, LANES), 1) % B_HEAD_DIM) // LSE_REP

    def scores(i):
        rows = slice(i * WINDOW_KEYS, (i + 1) * WINDOW_KEYS)
        k_prev = kp_ref[...] if i == 0 else kc_ref[(i - 1) * WINDOW_KEYS:i * WINDOW_KEYS, :]
        kk = jnp.concatenate([k_prev, kc_ref[rows, :]], axis=0)
        q = q_ref[rows, :]
        mask = first_mask if i == 0 else band
        out = []
        for grp in range(N_GROUPS):
            lanes = slice(grp * GROUP_LANES, (grp + 1) * GROUP_LANES)
            qg, kg = q[:, lanes], kk[:, lanes]
            for hh in range(HEADS_PER_GROUP):
                qm = jnp.where(qmask_ref[hh] > 0, qg, jnp.zeros_like(qg))
                sc = lax.dot_general(qm, kg, (((1,), (1,)), ((), ())), preferred_element_type=F32)
                out.append(jnp.where(mask, sc, NEG))
        return out

    def per_head(block):
        out = []
        for grp in range(N_GROUPS):
            vg = block[:, grp * GROUP_LANES:(grp + 1) * GROUP_LANES]
            out.append([jnp.where(vmask_ref[2 * hh * WINDOW_KEYS:(2 * hh + 1) * WINDOW_KEYS, :] > 0, vg,
                                  jnp.zeros_like(vg)) for hh in range(HEADS_PER_GROUP)])
        return out

    def finish(i, scs, v_prev):
        rows = slice(i * WINDOW_KEYS, (i + 1) * WINDOW_KEYS)
        v_cur = per_head(vc_ref[rows, :])
        head_sel = vmask_ref[...]
        lse_parts = []
        for grp in range(N_GROUPS):
            lanes = slice(grp * GROUP_LANES, (grp + 1) * GROUP_LANES)
            v_bd = jnp.concatenate([blk[grp][hh] for hh in range(HEADS_PER_GROUP) for blk in (v_prev, v_cur)],
                                   axis=0)
            ps, ms = [], []
            for hh in range(HEADS_PER_GROUP):
                sc = scs[grp * HEADS_PER_GROUP + hh]
                m = jnp.max(sc, axis=-1, keepdims=True)
                ps.append(jnp.exp(sc - m).astype(BF16))
                ms.append(m)
            p_cat = jnp.concatenate(ps, axis=1)
            pv = _dot(p_cat, v_bd)
            den = _dot(p_cat, head_sel)
            m_cat = jnp.broadcast_to(ms[-1], (WINDOW_KEYS, GROUP_LANES))
            for hh in range(HEADS_PER_GROUP - 1):
                m_cat = jnp.where(vlane == hh, ms[hh], m_cat)
            o_ref[rows, lanes] = (pv / den).astype(BF16)
            lse = m_cat + jnp.log(den)
            lse_parts += [lse[:, :LANES], lse[:, LANES:]]
        tile = lse_parts[-1]
        for part in range(len(lse_parts) - 1):
            tile = jnp.where(lse_pick == part, lse_parts[part], tile)
        lse_ref[rows, :] = tile
        return v_cur

    pending = scores(0)
    v_prev = per_head(vp_ref[...])
    for i in range(nsub):
        nxt = scores(i + 1) if i + 1 < nsub else None
        v_prev = finish(i, pending, v_prev)
        pending = nxt


def _attn_branch(q, k, v, tq):
    bn, dilation, sub_len, w = q.shape
    blocks_per_step = tq // WINDOW_KEYS
    cur = pl.BlockSpec((None, None, tq, w), lambda b, r, n: (b, r, n, 0))
    prev = pl.BlockSpec((None, None, WINDOW_KEYS, w),
                        lambda b, r, n: (b, r, jnp.maximum(n * blocks_per_step - 1, 0), 0))
    half = B_HEAD_DIM // 2
    lane = np.arange(GROUP_LANES)
    qmask = np.stack([np.broadcast_to(((lane % LANES) // half == hh), (WINDOW_KEYS, GROUP_LANES))
                      for hh in range(HEADS_PER_GROUP)]).astype(np.float32)
    vmask = np.concatenate([np.broadcast_to((lane // B_HEAD_DIM == hh), (2 * WINDOW_KEYS, GROUP_LANES))
                            for hh in range(HEADS_PER_GROUP)]).astype(np.float32)
    return pl.pallas_call(
        _attn_kernel,
        grid=(bn, dilation, sub_len // tq),
        in_specs=[_const_spec(qmask.shape), _const_spec(vmask.shape), cur, prev, cur, prev, cur],
        out_specs=[cur, pl.BlockSpec((None, None, tq, LANES), lambda b, r, n: (b, r, n, 0))],
        out_shape=[jax.ShapeDtypeStruct(q.shape, BF16),
                   jax.ShapeDtypeStruct((bn, dilation, sub_len, LANES), F32)],
        compiler_params=_cparams(3),
        name=f"attn_d{dilation}",
    )(jnp.asarray(qmask, BF16), jnp.asarray(vmask, BF16), q, k, k, v, v)


def _to_token_batch(x_ref, pad_ref, tb_ref):
    nb, nt, d = x_ref.shape
    for c in range(d // LANES):
        lanes = slice(c * LANES, (c + 1) * LANES)
        for b in range(nb):
            pad_ref[c, b * TB_PITCH:b * TB_PITCH + nt, :] = x_ref[b, :, lanes]
        for t in range(nt):
            tb_ref[t * nb:(t + 1) * nb, lanes] = pad_ref[c, pl.ds(t, nb, stride=TB_PITCH), :]


def _from_token_batch(tb_ref, pad_ref, out_ref):
    nb, nt, d = out_ref.shape
    for c in range(d // LANES):
        lanes = slice(c * LANES, (c + 1) * LANES)
        for t in range(nt):
            pad_ref[c, pl.ds(t, nb, stride=TB_PITCH), :] = tb_ref[t * nb:(t + 1) * nb, lanes]
        for b in range(nb):
            out_ref[b, :, lanes] = pad_ref[c, b * TB_PITCH:b * TB_PITCH + nt, :]


def _ffn(x1, nb, mod_ref, g_ref, up_ref, dww_ref, dwb_ref, down_ref, act_ref, carry_ref):
    rows = x1.shape[0]
    rep = lambda k: jnp.tile(mod_ref[k], (rows // nb, 1))
    h = _modulate(x1, g_ref[...], rep(3), rep(4)).astype(BF16)

    def up(c):
        return (_dot(h, up_ref[:, c * FFN_CHUNK:(c + 1) * FFN_CHUNK]),
                _dot(h, up_ref[:, FFN_DIM + c * FFN_CHUNK:FFN_DIM + (c + 1) * FFN_CHUNK]))

    def conv(z, col0):
        cols = slice(col0, col0 + FFN_CHUNK)
        hist = carry_ref[:, cols]
        carry_ref[:, cols] = z[rows - 2 * nb:, :]
        z1 = jnp.concatenate([hist[nb:], z[:rows - nb]], axis=0)
        z2 = jnp.concatenate([hist, z[:rows - 2 * nb]], axis=0)
        return (z * dww_ref[2:3, cols] + z1 * dww_ref[1:2, cols] + z2 * dww_ref[0:1, cols]
                + dwb_ref[:, cols])

    nch = FFN_DIM // FFN_CHUNK
    nxt = up(0)
    for c in range(nch):
        za, zb = nxt
        if c + 1 < nch:
            nxt = up(c + 1)
        a = conv(za, c * FFN_CHUNK)
        b = conv(zb, FFN_DIM + c * FFN_CHUNK)
        act_ref[:, c * FFN_CHUNK:(c + 1) * FFN_CHUNK] = (a * _sigmoid(a) * b).astype(BF16)
    return x1 + rep(5) * _dot(act_ref[...], down_ref[...])


def _ffn_weights(up_w, dw_w, dw_b, down_w):
    return up_w.astype(BF16), dw_w, dw_b.reshape(-1, 1, 2 * FFN_DIM), down_w.astype(BF16)


def _ffn_specs(layer):
    pick = lambda *shape: pl.BlockSpec((None,) + shape, lambda *_: (layer, 0, 0), pipeline_mode=pl.Buffered(1))
    return [pick(D_MODEL, 2 * FFN_DIM), pick(FFN_CONV_WIDTH, 2 * FFN_DIM), pick(1, 2 * FFN_DIM),
            pick(FFN_DIM, D_MODEL)]


def _ffn_scratch(rows, nb):
    return [pltpu.VMEM((rows, FFN_DIM), BF16), pltpu.VMEM(((FFN_CONV_WIDTH - 1) * nb, 2 * FFN_DIM), F32)]


def _pad_scratch(nb, d):
    return pltpu.VMEM((d // LANES, nb * TB_PITCH, LANES), F32)


def _ffn0_kernel(x_ref, mod_ref, g_ref, up_ref, dww_ref, dwb_ref, down_ref, out_ref,
                 pad_ref, tb_ref, act_ref, carry_ref):
    nb = x_ref.shape[0]

    @pl.when(pl.program_id(0) == 0)
    def _():
        carry_ref[...] = jnp.zeros_like(carry_ref)

    _to_token_batch(x_ref, pad_ref, tb_ref)
    y = _ffn(tb_ref[...], nb, mod_ref, g_ref, up_ref, dww_ref, dwb_ref, down_ref, act_ref, carry_ref)
    out_ref[...] = y.reshape(out_ref.shape)


def _ffn0(x, mod_t, norm_g, ffn_w):
    bn, s, d = x.shape
    nt = TB_TOKENS
    rows = nt * bn
    return pl.pallas_call(
        _ffn0_kernel,
        grid=(s // nt,),
        in_specs=[pl.BlockSpec((bn, nt, d), lambda i: (0, i, 0)), _const_spec((6, bn, d)),
                  _const_spec((1, d))] + _ffn_specs(0),
        out_specs=pl.BlockSpec((nt, bn, d), lambda i: (i, 0, 0)),
        out_shape=jax.ShapeDtypeStruct((s, bn, d), F32),
        scratch_shapes=[_pad_scratch(bn, d), pltpu.VMEM((rows, d), F32)] + _ffn_scratch(rows, bn),
        compiler_params=_cparams(1),
        name="ffn0",
    )(x, mod_t, norm_g.reshape(1, d), *ffn_w)


def _interleave4(src_ref, dst_ref):
    q4 = src_ref.shape[1]
    for c in range(src_ref.shape[2] // LANES):
        for r in range(4):
            dst_ref[c, pl.ds(r, q4, stride=4), :] = src_ref[r, :, c * LANES:(c + 1) * LANES].astype(F32)


def _interleave16(src_ref, mid_ref, dst_ref):
    q16 = src_ref.shape[1]
    q4 = 4 * q16
    for c in range(src_ref.shape[2] // LANES):
        for r in range(4):
            for r2 in range(4):
                mid_ref[c, pl.ds(r * q4 + r2, q16, stride=4), :] = (
                    src_ref[4 * r2 + r, :, c * LANES:(c + 1) * LANES].astype(F32))
        for r in range(4):
            dst_ref[c, pl.ds(r, q4, stride=4), :] = mid_ref[c, r * q4:(r + 1) * q4, :]


def _slabs(ref):
    return jnp.concatenate([ref[c] for c in range(ref.shape[0])], axis=1)


def _lse_lane(head):
    return B_HEAD_DIM * (head % 2) + LSE_REP * (head // 2)


def _mix_kernel(x_ref, mod_ref, ya_ref, o1_ref, l1_ref, o4_ref, l4_ref, o16_ref, l16_ref,
                exp_ref, wout_ref, out_ref, ilo_ref, mido_ref, ill_ref, midl_ref):
    _interleave4(o4_ref, ilo_ref.at[0])
    _interleave16(o16_ref, mido_ref, ilo_ref.at[1])
    _interleave4(l4_ref, ill_ref.at[0])
    _interleave16(l16_ref, midl_ref, ill_ref.at[1])
    l1, l2, l3 = l1_ref[...], ill_ref[0, 0], ill_ref[1, 0]
    mx = jnp.maximum(l1, jnp.maximum(l2, l3))
    e1, e2, e3 = jnp.exp(l1 - mx), jnp.exp(l2 - mx), jnp.exp(l3 - mx)
    tot = e1 + e2 + e3
    expand = exp_ref[...]
    yb = (_split_dot(e1 / tot, expand) * o1_ref[...].astype(F32)
          + _split_dot(e2 / tot, expand) * _slabs(ilo_ref.at[0])
          + _split_dot(e3 / tot, expand) * _slabs(ilo_ref.at[1]))
    y = _dot(ya_ref[0], wout_ref[0:A_WIDTH, :]) + _dot(yb.astype(BF16), wout_ref[A_WIDTH:, :])
    out_ref[0] = x_ref[0] + mod_ref[0, 2:3, :] * y


def _mix(x, mod, ya, branches, w_out):
    bn, s, d = x.shape
    tm = TOKEN_TILE
    tok = lambda width: pl.BlockSpec((1, tm, width), lambda b, i: (b, i, 0))
    branch_specs = []
    for dil in (1, 4, 16):
        for width in (B_WIDTH, LANES):
            if dil == 1:
                branch_specs.append(pl.BlockSpec((None, None, tm, width), lambda b, i: (b, 0, i, 0)))
            else:
                branch_specs.append(pl.BlockSpec((None, dil, tm // dil, width), lambda b, i: (b, 0, i, 0)))
    expand = np.zeros((LANES, B_WIDTH), np.float32)
    for head in range(B_HEADS):
        expand[_lse_lane(head), head * B_HEAD_DIM:(head + 1) * B_HEAD_DIM] = 1.0
    return pl.pallas_call(
        _mix_kernel,
        grid=(bn, s // tm),
        in_specs=[tok(d), pl.BlockSpec((1, 6, d), lambda b, i: (b, 0, 0)), tok(A_WIDTH)] + branch_specs
                 + [_const_spec((LANES, B_WIDTH)), _const_spec((A_WIDTH + B_WIDTH, d))],
        out_specs=tok(d),
        out_shape=jax.ShapeDtypeStruct((bn, s, d), F32),
        scratch_shapes=[pltpu.VMEM((2, SLABS, tm, LANES), F32), pltpu.VMEM((SLABS, tm, LANES), F32),
                        pltpu.VMEM((2, 1, tm, LANES), F32), pltpu.VMEM((1, tm, LANES), F32)],
        compiler_params=_cparams(2),
        name="mix",
    )(x, mod, ya, *branches, jnp.asarray(expand, BF16), w_out.astype(BF16))


def _convffn_kernel(x_ref, mod_ref, gm_ref, pw1_ref, pw1b_ref, dw_ref, dwb_ref, lng_ref, lnb_ref,
                    pw2_ref, pw2b_ref, g_ref, up_ref, fdww_ref, fdwb_ref, down_ref, out_ref,
                    ybuf_ref, cv_ref, pad_ref, tb_ref, act_ref, carry_ref):
    nt, nb, d = x_ref.shape
    rows = nt * nb
    hist = (CONV_WIDTH - 1) * nb

    @pl.when(pl.program_id(0) == 0)
    def _():
        carry_ref[...] = jnp.zeros_like(carry_ref)
        ybuf_ref[0:hist, :] = jnp.zeros((hist, d), F32)

    rep = lambda k: jnp.tile(mod_ref[k], (nt, 1))
    x = x_ref[...].reshape(rows, d)
    h = _modulate(x, gm_ref[...], rep(0), rep(1)).astype(BF16)

    def pw1(j):
        cols = slice(j * GLU_CHUNK, (j + 1) * GLU_CHUNK)
        gcols = slice(d + j * GLU_CHUNK, d + (j + 1) * GLU_CHUNK)
        return (_dot(h, pw1_ref[:, cols]) + pw1b_ref[:, cols],
                _dot(h, pw1_ref[:, gcols]) + pw1b_ref[:, gcols])

    rb = CONV_ROW_BLOCK
    n_glu = d // GLU_CHUNK
    nxt = pw1(0)
    for j in range(n_glu):
        a, gate = nxt
        if j + 1 < n_glu:
            nxt = pw1(j + 1)
        lanes = slice(j * GLU_CHUNK, (j + 1) * GLU_CHUNK)
        ybuf_ref[hist:hist + rows, lanes] = a * _sigmoid(gate)
        for r in range(rows // rb):
            acc = jnp.zeros((rb, GLU_CHUNK), F32) + dwb_ref[:, lanes]
            for k in range(CONV_WIDTH):
                start = r * rb + k * nb
                acc = acc + ybuf_ref[start:start + rb, lanes] * dw_ref[k:k + 1, lanes]
            cv_ref[r * rb:(r + 1) * rb, lanes] = acc
        ybuf_ref[0:hist, lanes] = ybuf_ref[rows:rows + hist, lanes]

    cv = cv_ref[...]
    mu = jnp.mean(cv, axis=-1, keepdims=True)
    var = jnp.mean(jnp.square(cv - mu), axis=-1, keepdims=True)
    yn = (cv - mu) * lax.rsqrt(var + EPS) * lng_ref[...] + lnb_ref[...]
    act = (yn * _sigmoid(yn)).astype(BF16)
    y = _dot(act, pw2_ref[...]) + pw2b_ref[...]
    x1 = x + rep(2) * y
    tb_ref[...] = _ffn(x1, nb, mod_ref, g_ref, up_ref, fdww_ref, fdwb_ref, down_ref, act_ref, carry_ref)
    _from_token_batch(tb_ref, pad_ref, out_ref)


def _convffn(x_tb, mod_t, norm_mix_g, pw1_w, pw1_b, dw_w, dw_b, ln_g, ln_b, pw2_w, pw2_b, norm_ffn_g, ffn_w):
    s, bn, d = x_tb.shape
    nt = TB_TOKENS
    rows = nt * bn
    row = lambda t: t.reshape(1, -1)
    return pl.pallas_call(
        _convffn_kernel,
        grid=(s // nt,),
        in_specs=[pl.BlockSpec((nt, bn, d), lambda i: (i, 0, 0)), _const_spec((6, bn, d)), _const_spec((1, d)),
                  _const_spec((d, 2 * d)), _const_spec((1, 2 * d)), _const_spec((CONV_WIDTH, d)),
                  _const_spec((1, d)), _const_spec((1, d)), _const_spec((1, d)),
                  _const_spec((d, d)), _const_spec((1, d)), _const_spec((1, d))] + _ffn_specs(1),
        out_specs=pl.BlockSpec((bn, nt, d), lambda i: (0, i, 0)),
        out_shape=jax.ShapeDtypeStruct((bn, s, d), F32),
        scratch_shapes=[pltpu.VMEM((rows + (CONV_WIDTH - 1) * bn, d), F32), pltpu.VMEM((rows, d), F32),
                        _pad_scratch(bn, d), pltpu.VMEM((rows, d), F32)] + _ffn_scratch(rows, bn),
        compiler_params=_cparams(1),
        name="convffn",
    )(x_tb, mod_t, row(norm_mix_g), pw1_w.astype(BF16), row(pw1_b), dw_w, row(dw_b), row(ln_g), row(ln_b),
      pw2_w.astype(BF16), row(pw2_b), row(norm_ffn_g), *ffn_w)


def kernel(x, c, positions, ada_w, ada_b, norm_mix_g, norm_ffn_g, ab_w_in, a_vnorm_g, a_spatial_w, a_spatial_b, b_q_norm_g, b_k_norm_g, ab_w_out, conv_pw1_w, conv_pw1_b, conv_dw_w, conv_dw_b, conv_ln_g, conv_ln_b, conv_pw2_w, conv_pw2_b, ffn_up_w, ffn_dw_w, ffn_dw_b, ffn_down_w):
    bn, s, d = x.shape
    mod = _ada(c, ada_w, ada_b).reshape(ada_w.shape[0], bn, 6, d)
    cos, sin = _rope_tables(positions)

    ya, qkv = _inproj(x, mod[0], norm_mix_g[0], ab_w_in[0], a_vnorm_g[0], a_spatial_w[0],
                      a_spatial_b[0], b_q_norm_g[0], b_k_norm_g[0], cos, sin)
    branches = []
    for window, dilation in DILATED_PATTERNS:
        assert window // dilation == WINDOW_KEYS
        tq = min(ATTN_QUERY_TILE, s // dilation)
        branches += _attn_branch(*qkv[dilation], tq)
    x = _mix(x, mod[0], ya, branches, ab_w_out[0])
    mod_t = jnp.swapaxes(mod, 1, 2)
    ffn_w = _ffn_weights(ffn_up_w, ffn_dw_w, ffn_dw_b, ffn_down_w)
    x_tb = _ffn0(x, mod_t[0], norm_ffn_g[0], ffn_w)

    return _convffn(x_tb, mod_t[1], norm_mix_g[1], conv_pw1_w[0], conv_pw1_b[0], conv_dw_w[0], conv_dw_b[0],
                    conv_ln_g[0], conv_ln_b[0], conv_pw2_w[0], conv_pw2_b[0], norm_ffn_g[1], ffn_w)
```

```python
import math

import numpy as np
import jax
import jax.numpy as jnp
from jax import lax
from jax.experimental import pallas as pl
from jax.experimental.pallas import tpu as pltpu

F32 = jnp.float32
BF16 = jnp.bfloat16

D_MODEL = 1024
A_WIDTH = 512
A_GROUPS = 4
A_GROUP_DIM = 128
CHUNK = 128
B_WIDTH = 512
B_HEAD_DIM = 64
B_HEADS = 8
HEADS_PER_GROUP = 4
GROUP_LANES = HEADS_PER_GROUP * B_HEAD_DIM
N_GROUPS = B_HEADS // HEADS_PER_GROUP
DILATED_PATTERNS = ((128, 1), (512, 4), (2048, 16))
WINDOW_KEYS = 128
ATTN_QUERY_TILE = 1024
ROPE_THETA = 10000.0
CONV_WIDTH = 31
CONV_ROW_BLOCK = 64
GLU_CHUNK = 256
FFN_DIM = 2816
FFN_CONV_WIDTH = 3
FFN_CHUNK = 256
ROPE_ROW_BLOCK = 1024
ADA_COL_BLOCK = 1536
LSE_REP = 16
TB_TOKENS = 64
TB_PITCH = 72
EPS = 1e-6
NEG = -1e30

LANES = 128
SLABS = B_WIDTH // LANES
TOKEN_TILE = 512
V7X_VMEM_LIMIT_BYTES = 56 * 1024 * 1024


def _cparams(n_axes):
    return pltpu.CompilerParams(
        dimension_semantics=("arbitrary",) * n_axes,
        vmem_limit_bytes=V7X_VMEM_LIMIT_BYTES)


def _const_spec(shape):
    nd = len(shape)
    return pl.BlockSpec(shape, lambda *_: (0,) * nd, pipeline_mode=pl.Buffered(1))


def _dot(a, b):
    return jnp.dot(a, b, preferred_element_type=F32)


def _split_dot(a, b2):
    hi = a.astype(BF16)
    lo = (a - hi.astype(F32)).astype(BF16)
    return _dot(jnp.concatenate([hi, lo], axis=1), b2)


def _sigmoid(x):
    return jax.nn.sigmoid(x)


def _gelu(x):
    return 0.5 * x * (1.0 + lax.erf(x * (1.0 / math.sqrt(2.0))))


def _modulate(x, g, shift, scale):
    ms = jnp.mean(x * x, axis=-1, keepdims=True)
    return (x * lax.rsqrt(ms + EPS) * g) * (1.0 + scale) + shift


def _ada_kernel(c_ref, w_ref, b_ref, o_ref):
    c = c_ref[...]
    act = (c * _sigmoid(c)).astype(BF16)
    o_ref[0] = _dot(act, w_ref[0].astype(BF16)) + b_ref[0]


def _ada(c, ada_w, ada_b):
    depth, d, n = ada_w.shape
    bn = c.shape[0]
    nb = ADA_COL_BLOCK
    return pl.pallas_call(
        _ada_kernel,
        grid=(depth, n // nb),
        in_specs=[
            pl.BlockSpec((bn, d), lambda l, j: (0, 0)),
            pl.BlockSpec((1, d, nb), lambda l, j: (l, 0, j)),
            pl.BlockSpec((1, 1, nb), lambda l, j: (l, 0, j)),
        ],
        out_specs=pl.BlockSpec((1, bn, nb), lambda l, j: (l, 0, j)),
        out_shape=jax.ShapeDtypeStruct((depth, bn, n), F32),
        compiler_params=_cparams(2),
        name="ada",
    )(c, ada_w, ada_b.reshape(depth, 1, n))


def _rope_kernel(pos_ref, inv_ref, cos_ref, sin_ref):
    rb = pos_ref.shape[0]
    half = B_HEAD_DIM // 2
    per_row = LANES // half
    ang = pos_ref[...].astype(F32) * inv_ref[...]
    cos, sin = jnp.cos(ang), jnp.sin(ang)
    src = lax.broadcasted_iota(jnp.int32, (2 * LANES, LANES), 0) % LANES
    dst = lax.broadcasted_iota(jnp.int32, (2 * LANES, LANES), 1)
    for j in range(per_row):
        spread = jnp.where((src // half == j) & (src % half == dst % half), 1.0, 0.0).astype(BF16)
        cos_ref[pl.ds(j, rb, stride=per_row), :] = _split_dot(cos, spread)
        sin_ref[pl.ds(j, rb, stride=per_row), :] = _split_dot(sin, spread)


def _rope_tables(positions):
    bn, s = positions.shape
    half = B_HEAD_DIM // 2
    inv_freq = 1.0 / (ROPE_THETA ** (jnp.arange(0, B_HEAD_DIM, 2, dtype=F32) / B_HEAD_DIM))
    per_row = LANES // half
    rows = bn * s // per_row
    pos_rep = jnp.repeat(positions.reshape(rows, per_row), half, axis=1)
    inv = jnp.tile(inv_freq, per_row).reshape(1, LANES)
    rb = min(rows, ROPE_ROW_BLOCK)
    cos, sin = pl.pallas_call(
        _rope_kernel,
        grid=(rows // rb,),
        in_specs=[pl.BlockSpec((rb, LANES), lambda i: (i, 0)),
                  pl.BlockSpec((1, LANES), lambda i: (0, 0))],
        out_specs=[pl.BlockSpec((per_row * rb, LANES), lambda i: (i, 0))] * 2,
        out_shape=[jax.ShapeDtypeStruct((bn * s, LANES), F32)] * 2,
        compiler_params=_cparams(1),
        name="rope",
    )(pos_rep, inv)
    return cos.reshape(bn, s, LANES), sin.reshape(bn, s, LANES)


def _qk_perm():
    half = B_HEAD_DIM // 2
    perm = np.zeros(B_WIDTH, np.int32)
    for grp in range(N_GROUPS):
        for n in range(GROUP_LANES):
            head = grp * HEADS_PER_GROUP + (n % LANES) // half
            perm[grp * GROUP_LANES + n] = head * B_HEAD_DIM + (n // LANES) * half + n % half
    return perm


def _deinterleave(tok_ref, r4_ref, out1_ref, out4_ref, out16_ref):
    tm = tok_ref.shape[1]
    q4, q16 = tm // 4, tm // 16
    for c in range(SLABS):
        lanes = slice(c * LANES, (c + 1) * LANES)
        out1_ref[0, :, lanes] = tok_ref[c].astype(BF16)
        for r in range(4):
            blk = tok_ref[c, pl.ds(r, q4, stride=4), :]
            out4_ref[0, r, :, lanes] = blk.astype(BF16)
            r4_ref[c, r * q4:(r + 1) * q4, :] = blk
        for r in range(4):
            for r2 in range(4):
                blk = r4_ref[c, pl.ds(r * q4 + r2, q16, stride=4), :]
                out16_ref[0, 4 * r2 + r, :, lanes] = blk.astype(BF16)


def _inproj_kernel(x_ref, mod_ref, g_ref, w_ref, vg_ref, ws_ref, wsb_ref, qg_ref, kg_ref,
                   cos_ref, sin_ref, ya_ref, q1_ref, k1_ref, v1_ref, q4_ref, k4_ref, v4_ref,
                   q16_ref, k16_ref, v16_ref, tok_ref, r4_ref):
    tm = x_ref.shape[1]
    x = x_ref[0]
    h = _modulate(x, g_ref[...], mod_ref[0, 0:1, :], mod_ref[0, 1:2, :]).astype(BF16)

    section = lambda k: _dot(h, w_ref[:, k * A_WIDTH:(k + 1) * A_WIDTH])
    z_ua, z_va = section(0), section(1)

    ua = _gelu(z_ua)
    z_q = section(2)
    va = _gelu(z_va)
    row = lax.broadcasted_iota(jnp.int32, (CHUNK, CHUNK), 0)
    col = lax.broadcasted_iota(jnp.int32, (CHUNK, CHUNK), 1)
    causal = row >= col
    vns = []
    for g in range(A_GROUPS):
        lanes = slice(g * A_GROUP_DIM, (g + 1) * A_GROUP_DIM)
        vg = va[:, lanes]
        mu = jnp.mean(vg, axis=-1, keepdims=True)
        var = jnp.mean(jnp.square(vg - mu), axis=-1, keepdims=True)
        vns.append(((vg - mu) * lax.rsqrt(var + EPS) * vg_ref[:, lanes]).astype(BF16))
    z_k = section(3)
    for g in range(A_GROUPS):
        lanes = slice(g * A_GROUP_DIM, (g + 1) * A_GROUP_DIM)
        ws = jnp.where(causal, ws_ref[g], 0.0).astype(BF16)
        bias = wsb_ref[:, g:g + 1]
        for c in range(tm // CHUNK):
            rows = slice(c * CHUNK, (c + 1) * CHUNK)
            f = _dot(ws, vns[g][rows, :]) + bias
            ya_ref[0, rows, lanes] = (ua[rows, lanes] * f).astype(BF16)
    z_v = section(4)

    seg_r = lax.broadcasted_iota(jnp.int32, (2 * LANES, LANES), 0) % LANES // (B_HEAD_DIM // 2)
    seg_c = lax.broadcasted_iota(jnp.int32, (2 * LANES, LANES), 1) // (B_HEAD_DIM // 2)
    seg_ones = jnp.where(seg_r == seg_c, 1.0, 0.0).astype(BF16)
    cos = cos_ref[0]
    sin = sin_ref[0]

    def norm_rot(z, gain_ref, tok, out_scale):
        for grp in range(N_GROUPS):
            lo = grp * GROUP_LANES
            z1 = z[:, lo:lo + LANES]
            z2 = z[:, lo + LANES:lo + 2 * LANES]
            ss = _split_dot(z1 * z1 + z2 * z2, seg_ones)
            inv = lax.rsqrt(ss * (1.0 / B_HEAD_DIM) + EPS)
            a1 = z1 * inv * gain_ref[:, lo:lo + LANES]
            a2 = z2 * inv * gain_ref[:, lo + LANES:lo + 2 * LANES]
            tok[2 * grp] = (a1 * cos - a2 * sin) * out_scale
            tok[2 * grp + 1] = (a2 * cos + a1 * sin) * out_scale

    norm_rot(z_q, qg_ref, tok_ref.at[0], B_HEAD_DIM ** -0.5)
    _deinterleave(tok_ref.at[0], r4_ref.at[0], q1_ref, q4_ref, q16_ref)
    norm_rot(z_k, kg_ref, tok_ref.at[1], 1.0)
    _deinterleave(tok_ref.at[1], r4_ref.at[1], k1_ref, k4_ref, k16_ref)
    for c in range(SLABS):
        tok_ref[2, c] = z_v[:, c * LANES:(c + 1) * LANES]
    _deinterleave(tok_ref.at[2], r4_ref.at[2], v1_ref, v4_ref, v16_ref)


def _inproj(x, mod, norm_g, w_in, a_vnorm_g, a_spatial_w, a_spatial_b, q_g, k_g, cos, sin):
    bn, s, d = x.shape
    tm = TOKEN_TILE
    perm = _qk_perm()
    qs = 2 * A_WIDTH
    w = jnp.concatenate([w_in[:, :qs], w_in[:, qs:qs + B_WIDTH][:, perm],
                         w_in[:, qs + B_WIDTH:qs + 2 * B_WIDTH][:, perm],
                         w_in[:, qs + 2 * B_WIDTH:]], axis=1).astype(BF16)
    half = B_HEAD_DIM // 2

    def gain_row(g):
        grp = jnp.concatenate([jnp.tile(g[:half], HEADS_PER_GROUP), jnp.tile(g[half:], HEADS_PER_GROUP)])
        return jnp.tile(grp, N_GROUPS).reshape(1, B_WIDTH)

    tok = lambda width: pl.BlockSpec((1, tm, width), lambda b, i: (b, i, 0))
    res = lambda dil: pl.BlockSpec((1, dil, tm // dil, B_WIDTH), lambda b, i: (b, 0, i, 0))
    sds1 = jax.ShapeDtypeStruct((bn, s, B_WIDTH), BF16)
    sds = lambda dil: jax.ShapeDtypeStruct((bn, dil, s // dil, B_WIDTH), BF16)
    outs = pl.pallas_call(
        _inproj_kernel,
        grid=(bn, s // tm),
        in_specs=[
            tok(d),
            pl.BlockSpec((1, 6, d), lambda b, i: (b, 0, 0)),
            _const_spec((1, d)),
            _const_spec(w.shape),
            _const_spec((1, A_WIDTH)),
            _const_spec((A_GROUPS, CHUNK, CHUNK)),
            _const_spec((CHUNK, A_GROUPS)),
            _const_spec((1, B_WIDTH)),
            _const_spec((1, B_WIDTH)),
            tok(LANES),
            tok(LANES),
        ],
        out_specs=[tok(A_WIDTH)] + [tok(B_WIDTH)] * 3 + [res(4)] * 3 + [res(16)] * 3,
        out_shape=[jax.ShapeDtypeStruct((bn, s, A_WIDTH), BF16)] + [sds1] * 3 + [sds(4)] * 3 + [sds(16)] * 3,
        scratch_shapes=[pltpu.VMEM((3, SLABS, tm, LANES), F32), pltpu.VMEM((3, SLABS, tm, LANES), F32)],
        compiler_params=_cparams(2),
        name="inproj",
    )(x, mod, norm_g.reshape(1, d), w, a_vnorm_g.reshape(1, A_WIDTH), a_spatial_w,
      a_spatial_b.T, gain_row(q_g), gain_row(k_g), cos, sin)
    ya = outs[0]
    qkv = {1: [t.reshape(bn, 1, s, B_WIDTH) for t in outs[1:4]], 4: outs[4:7], 16: outs[7:10]}
    return ya, qkv


def _attn_kernel(qmask_ref, vmask_ref, q_ref, kp_ref, kc_ref, vp_ref, vc_ref, o_ref, lse_ref):
    tq = q_ref.shape[0]
    nsub = tq // WINDOW_KEYS
    n = pl.program_id(2)
    qi = lax.broadcasted_iota(jnp.int32, (WINDOW_KEYS, 2 * WINDOW_KEYS), 0)
    kj = lax.broadcasted_iota(jnp.int32, (WINDOW_KEYS, 2 * WINDOW_KEYS), 1)
    dist = qi + WINDOW_KEYS - kj
    band = (dist >= 0) & (dist <= WINDOW_KEYS)
    first_mask = band & ((kj >= WINDOW_KEYS) | (n > 0))
    vlane = lax.broadcasted_iota(jnp.int32, (WINDOW_KEYS, GROUP_LANES), 1) // B_HEAD_DIM
    lse_pick = (lax.broadcasted_iota(jnp.int32, (WINDOW_KEYS, LANES), 1) % B_HEAD_DIM) // LSE_REP

    def scores(i):
        rows = slice(i * WINDOW_KEYS, (i + 1) * WINDOW_KEYS)
        k_prev = kp_ref[...] if i == 0 else kc_ref[(i - 1) * WINDOW_KEYS:i * WINDOW_KEYS, :]
        kk = jnp.concatenate([k_prev, kc_ref[rows, :]], axis=0)
        q = q_ref[rows, :]
        mask = first_mask if i == 0 else band
        out = []
        for grp in range(N_GROUPS):
            lanes = slice(grp * GROUP_LANES, (grp + 1) * GROUP_LANES)
            qg, kg = q[:, lanes], kk[:, lanes]
            for hh in range(HEADS_PER_GROUP):
                qm = jnp.where(qmask_ref[hh] > 0, qg, jnp.zeros_like(qg))
                sc = lax.dot_general(qm, kg, (((1,), (1,)), ((), ())), preferred_element_type=F32)
                out.append(jnp.where(mask, sc, NEG))
        return out

    def per_head(block):
        out = []
        for grp in range(N_GROUPS):
            vg = block[:, grp * GROUP_LANES:(grp + 1) * GROUP_LANES]
            out.append([jnp.where(vmask_ref[2 * hh * WINDOW_KEYS:(2 * hh + 1) * WINDOW_KEYS, :] > 0, vg,
                                  jnp.zeros_like(vg)) for hh in range(HEADS_PER_GROUP)])
        return out

    def finish(i, scs, v_prev):
        rows = slice(i * WINDOW_KEYS, (i + 1) * WINDOW_KEYS)
        v_cur = per_head(vc_ref[rows, :])
        head_sel = vmask_ref[...]
        lse_parts = []
        for grp in range(N_GROUPS):
            lanes = slice(grp * GROUP_LANES, (grp + 1) * GROUP_LANES)
            v_bd = jnp.concatenate([blk[grp][hh] for hh in range(HEADS_PER_GROUP) for blk in (v_prev, v_cur)],
                                   axis=0)
            ps, ms = [], []
            for hh in range(HEADS_PER_GROUP):
                sc = scs[grp * HEADS_PER_GROUP + hh]
                m = jnp.max(sc, axis=-1, keepdims=True)
                ps.append(jnp.exp(sc - m).astype(BF16))
                ms.append(m)
            p_cat = jnp.concatenate(ps, axis=1)
            pv = _dot(p_cat, v_bd)
            den = _dot(p_cat, head_sel)
            m_cat = jnp.broadcast_to(ms[-1], (WINDOW_KEYS, GROUP_LANES))
            for hh in range(HEADS_PER_GROUP - 1):
                m_cat = jnp.where(vlane == hh, ms[hh], m_cat)
            o_ref[rows, lanes] = (pv / den).astype(BF16)
            lse = m_cat + jnp.log(den)
            lse_parts += [lse[:, :LANES], lse[:, LANES:]]
        tile = lse_parts[-1]
        for part in range(len(lse_parts) - 1):
            tile = jnp.where(lse_pick == part, lse_parts[part], tile)
        lse_ref[rows, :] = tile
        return v_cur

    pending = scores(0)
    v_prev = per_head(vp_ref[...])
    for i in range(nsub):
        nxt = scores(i + 1) if i + 1 < nsub else None
        v_prev = finish(i, pending, v_prev)
        pending = nxt


def _attn_branch(q, k, v, tq):
    bn, dilation, sub_len, w = q.shape
    blocks_per_step = tq // WINDOW_KEYS
    cur = pl.BlockSpec((None, None, tq, w), lambda b, r, n: (b, r, n, 0))
    prev = pl.BlockSpec((None, None, WINDOW_KEYS, w),
                        lambda b, r, n: (b, r, jnp.maximum(n * blocks_per_step - 1, 0), 0))
    half = B_HEAD_DIM // 2
    lane = np.arange(GROUP_LANES)
    qmask = np.stack([np.broadcast_to(((lane % LANES) // half == hh), (WINDOW_KEYS, GROUP_LANES))
                      for hh in range(HEADS_PER_GROUP)]).astype(np.float32)
    vmask = np.concatenate([np.broadcast_to((lane // B_HEAD_DIM == hh), (2 * WINDOW_KEYS, GROUP_LANES))
                            for hh in range(HEADS_PER_GROUP)]).astype(np.float32)
    return pl.pallas_call(
        _attn_kernel,
        grid=(bn, dilation, sub_len // tq),
        in_specs=[_const_spec(qmask.shape), _const_spec(vmask.shape), cur, prev, cur, prev, cur],
        out_specs=[cur, pl.BlockSpec((None, None, tq, LANES), lambda b, r, n: (b, r, n, 0))],
        out_shape=[jax.ShapeDtypeStruct(q.shape, BF16),
                   jax.ShapeDtypeStruct((bn, dilation, sub_len, LANES), F32)],
        compiler_params=_cparams(3),
        name=f"attn_d{dilation}",
    )(jnp.asarray(qmask, BF16), jnp.asarray(vmask, BF16), q, k, k, v, v)


def _to_token_batch(x_ref, pad_ref, tb_ref):
    nb, nt, d = x_ref.shape
    for c in range(d // LANES):
        lanes = slice(c * LANES, (c + 1) * LANES)
        for b in range(nb):
            pad_ref[c, b * TB_PITCH:b * TB_PITCH + nt, :] = x_ref[b, :, lanes]
        for t in range(nt):
            tb_ref[t * nb:(t + 1) * nb, lanes] = pad_ref[c, pl.ds(t, nb, stride=TB_PITCH), :]


def _from_token_batch(tb_ref, pad_ref, out_ref):
    nb, nt, d = out_ref.shape
    for c in range(d // LANES):
        lanes = slice(c * LANES, (c + 1) * LANES)
        for t in range(nt):
            pad_ref[c, pl.ds(t, nb, stride=TB_PITCH), :] = tb_ref[t * nb:(t + 1) * nb, lanes]
        for b in range(nb):
            out_ref[b, :, lanes] = pad_ref[c, b * TB_PITCH:b * TB_PITCH + nt, :]


def _ffn(x1, nb, mod_ref, g_ref, up_ref, dww_ref, dwb_ref, down_ref, act_ref, carry_ref):
    rows = x1.shape[0]
    rep = lambda k: jnp.tile(mod_ref[k], (rows // nb, 1))
    h = _modulate(x1, g_ref[...], rep(3), rep(4)).astype(BF16)

    def up(c):
        return (_dot(h, up_ref[:, c * FFN_CHUNK:(c + 1) * FFN_CHUNK]),
                _dot(h, up_ref[:, FFN_DIM + c * FFN_CHUNK:FFN_DIM + (c + 1) * FFN_CHUNK]))

    def conv(z, col0):
        cols = slice(col0, col0 + FFN_CHUNK)
        hist = carry_ref[:, cols]
        carry_ref[:, cols] = z[rows - 2 * nb:, :]
        z1 = jnp.concatenate([hist[nb:], z[:rows - nb]], axis=0)
        z2 = jnp.concatenate([hist, z[:rows - 2 * nb]], axis=0)
        return (z * dww_ref[2:3, cols] + z1 * dww_ref[1:2, cols] + z2 * dww_ref[0:1, cols]
                + dwb_ref[:, cols])

    nch = FFN_DIM // FFN_CHUNK
    nxt = up(0)
    for c in range(nch):
        za, zb = nxt
        if c + 1 < nch:
            nxt = up(c + 1)
        a = conv(za, c * FFN_CHUNK)
        b = conv(zb, FFN_DIM + c * FFN_CHUNK)
        act_ref[:, c * FFN_CHUNK:(c + 1) * FFN_CHUNK] = (a * _sigmoid(a) * b).astype(BF16)
    return x1 + rep(5) * _dot(act_ref[...], down_ref[...])


def _ffn_weights(up_w, dw_w, dw_b, down_w):
    return up_w.astype(BF16), dw_w, dw_b.reshape(-1, 1, 2 * FFN_DIM), down_w.astype(BF16)


def _ffn_specs(layer):
    pick = lambda *shape: pl.BlockSpec((None,) + shape, lambda *_: (layer, 0, 0), pipeline_mode=pl.Buffered(1))
    return [pick(D_MODEL, 2 * FFN_DIM), pick(FFN_CONV_WIDTH, 2 * FFN_DIM), pick(1, 2 * FFN_DIM),
            pick(FFN_DIM, D_MODEL)]


def _ffn_scratch(rows, nb):
    return [pltpu.VMEM((rows, FFN_DIM), BF16), pltpu.VMEM(((FFN_CONV_WIDTH - 1) * nb, 2 * FFN_DIM), F32)]


def _pad_scratch(nb, d):
    return pltpu.VMEM((d // LANES, nb * TB_PITCH, LANES), F32)


def _ffn0_kernel(x_ref, mod_ref, g_ref, up_ref, dww_ref, dwb_ref, down_ref, out_ref,
                 pad_ref, tb_ref, act_ref, carry_ref):
    nb = x_ref.shape[0]

    @pl.when(pl.program_id(0) == 0)
    def _():
        carry_ref[...] = jnp.zeros_like(carry_ref)

    _to_token_batch(x_ref, pad_ref, tb_ref)
    y = _ffn(tb_ref[...], nb, mod_ref, g_ref, up_ref, dww_ref, dwb_ref, down_ref, act_ref, carry_ref)
    out_ref[...] = y.reshape(out_ref.shape)


def _ffn0(x, mod_t, norm_g, ffn_w):
    bn, s, d = x.shape
    nt = TB_TOKENS
    rows = nt * bn
    return pl.pallas_call(
        _ffn0_kernel,
        grid=(s // nt,),
        in_specs=[pl.BlockSpec((bn, nt, d), lambda i: (0, i, 0)), _const_spec((6, bn, d)),
                  _const_spec((1, d))] + _ffn_specs(0),
        out_specs=pl.BlockSpec((nt, bn, d), lambda i: (i, 0, 0)),
        out_shape=jax.ShapeDtypeStruct((s, bn, d), F32),
        scratch_shapes=[_pad_scratch(bn, d), pltpu.VMEM((rows, d), F32)] + _ffn_scratch(rows, bn),
        compiler_params=_cparams(1),
        name="ffn0",
    )(x, mod_t, norm_g.reshape(1, d), *ffn_w)


def _interleave4(src_ref, dst_ref):
    q4 = src_ref.shape[1]
    for c in range(src_ref.shape[2] // LANES):
        for r in range(4):
            dst_ref[c, pl.ds(r, q4, stride=4), :] = src_ref[r, :, c * LANES:(c + 1) * LANES].astype(F32)


def _interleave16(src_ref, mid_ref, dst_ref):
    q16 = src_ref.shape[1]
    q4 = 4 * q16
    for c in range(src_ref.shape[2] // LANES):
        for r in range(4):
            for r2 in range(4):
                mid_ref[c, pl.ds(r * q4 + r2, q16, stride=4), :] = (
                    src_ref[4 * r2 + r, :, c * LANES:(c + 1) * LANES].astype(F32))
        for r in range(4):
            dst_ref[c, pl.ds(r, q4, stride=4), :] = mid_ref[c, r * q4:(r + 1) * q4, :]


def _slabs(ref):
    return jnp.concatenate([ref[c] for c in range(ref.shape[0])], axis=1)


def _lse_lane(head):
    return B_HEAD_DIM * (head % 2) + LSE_REP * (head // 2)


def _mix_kernel(x_ref, mod_ref, ya_ref, o1_ref, l1_ref, o4_ref, l4_ref, o16_ref, l16_ref,
                exp_ref, wout_ref, out_ref, ilo_ref, mido_ref, ill_ref, midl_ref):
    _interleave4(o4_ref, ilo_ref.at[0])
    _interleave16(o16_ref, mido_ref, ilo_ref.at[1])
    _interleave4(l4_ref, ill_ref.at[0])
    _interleave16(l16_ref, midl_ref, ill_ref.at[1])
    l1, l2, l3 = l1_ref[...], ill_ref[0, 0], ill_ref[1, 0]
    mx = jnp.maximum(l1, jnp.maximum(l2, l3))
    e1, e2, e3 = jnp.exp(l1 - mx), jnp.exp(l2 - mx), jnp.exp(l3 - mx)
    tot = e1 + e2 + e3
    expand = exp_ref[...]
    yb = (_split_dot(e1 / tot, expand) * o1_ref[...].astype(F32)
          + _split_dot(e2 / tot, expand) * _slabs(ilo_ref.at[0])
          + _split_dot(e3 / tot, expand) * _slabs(ilo_ref.at[1]))
    y = _dot(ya_ref[0], wout_ref[0:A_WIDTH, :]) + _dot(yb.astype(BF16), wout_ref[A_WIDTH:, :])
    out_ref[0] = x_ref[0] + mod_ref[0, 2:3, :] * y


def _mix(x, mod, ya, branches, w_out):
    bn, s, d = x.shape
    tm = TOKEN_TILE
    tok = lambda width: pl.BlockSpec((1, tm, width), lambda b, i: (b, i, 0))
    branch_specs = []
    for dil in (1, 4, 16):
        for width in (B_WIDTH, LANES):
            if dil == 1:
                branch_specs.append(pl.BlockSpec((None, None, tm, width), lambda b, i: (b, 0, i, 0)))
            else:
                branch_specs.append(pl.BlockSpec((None, dil, tm // dil, width), lambda b, i: (b, 0, i, 0)))
    expand = np.zeros((2 * LANES, B_WIDTH), np.float32)
    for head in range(B_HEADS):
        expand[_lse_lane(head)::LANES, head * B_HEAD_DIM:(head + 1) * B_HEAD_DIM] = 1.0
    return pl.pallas_call(
        _mix_kernel,
        grid=(bn, s // tm),
        in_specs=[tok(d), pl.BlockSpec((1, 6, d), lambda b, i: (b, 0, 0)), tok(A_WIDTH)] + branch_specs
                 + [_const_spec((2 * LANES, B_WIDTH)), _const_spec((A_WIDTH + B_WIDTH, d))],
        out_specs=tok(d),
        out_shape=jax.ShapeDtypeStruct((bn, s, d), F32),
        scratch_shapes=[pltpu.VMEM((2, SLABS, tm, LANES), F32), pltpu.VMEM((SLABS, tm, LANES), F32),
                        pltpu.VMEM((2, 1, tm, LANES), F32), pltpu.VMEM((1, tm, LANES), F32)],
        compiler_params=_cparams(2),
        name="mix",
    )(x, mod, ya, *branches, jnp.asarray(expand, BF16), w_out.astype(BF16))


def _convffn_kernel(x_ref, mod_ref, gm_ref, pw1_ref, pw1b_ref, dw_ref, dwb_ref, lng_ref, lnb_ref,
                    pw2_ref, pw2b_ref, g_ref, up_ref, fdww_ref, fdwb_ref, down_ref, out_ref,
                    ybuf_ref, cv_ref, pad_ref, tb_ref, act_ref, carry_ref):
    nt, nb, d = x_ref.shape
    rows = nt * nb
    hist = (CONV_WIDTH - 1) * nb

    @pl.when(pl.program_id(0) == 0)
    def _():
        carry_ref[...] = jnp.zeros_like(carry_ref)
        ybuf_ref[0:hist, :] = jnp.zeros((hist, d), F32)

    rep = lambda k: jnp.tile(mod_ref[k], (nt, 1))
    x = x_ref[...].reshape(rows, d)
    h = _modulate(x, gm_ref[...], rep(0), rep(1)).astype(BF16)

    def pw1(j):
        cols = slice(j * GLU_CHUNK, (j + 1) * GLU_CHUNK)
        gcols = slice(d + j * GLU_CHUNK, d + (j + 1) * GLU_CHUNK)
        return (_dot(h, pw1_ref[:, cols]) + pw1b_ref[:, cols],
                _dot(h, pw1_ref[:, gcols]) + pw1b_ref[:, gcols])

    rb = CONV_ROW_BLOCK
    n_glu = d // GLU_CHUNK
    nxt = pw1(0)
    for j in range(n_glu):
        a, gate = nxt
        if j + 1 < n_glu:
            nxt = pw1(j + 1)
        lanes = slice(j * GLU_CHUNK, (j + 1) * GLU_CHUNK)
        ybuf_ref[hist:hist + rows, lanes] = a * _sigmoid(gate)
        for r in range(rows // rb):
            acc = jnp.zeros((rb, GLU_CHUNK), F32) + dwb_ref[:, lanes]
            for k in range(CONV_WIDTH):
                start = r * rb + k * nb
                acc = acc + ybuf_ref[start:start + rb, lanes] * dw_ref[k:k + 1, lanes]
            cv_ref[r * rb:(r + 1) * rb, lanes] = acc
        ybuf_ref[0:hist, lanes] = ybuf_ref[rows:rows + hist, lanes]

    cv = cv_ref[...]
    mu = jnp.mean(cv, axis=-1, keepdims=True)
    var = jnp.mean(jnp.square(cv - mu), axis=-1, keepdims=True)
    yn = (cv - mu) * lax.rsqrt(var + EPS) * lng_ref[...] + lnb_ref[...]
    act = (yn * _sigmoid(yn)).astype(BF16)
    y = _dot(act, pw2_ref[...]) + pw2b_ref[...]
    x1 = x + rep(2) * y
    tb_ref[...] = _ffn(x1, nb, mod_ref, g_ref, up_ref, fdww_ref, fdwb_ref, down_ref, act_ref, carry_ref)
    _from_token_batch(tb_ref, pad_ref, out_ref)


def _convffn(x_tb, mod_t, norm_mix_g, pw1_w, pw1_b, dw_w, dw_b, ln_g, ln_b, pw2_w, pw2_b, norm_ffn_g, ffn_w):
    s, bn, d = x_tb.shape
    nt = TB_TOKENS
    rows = nt * bn
    row = lambda t: t.reshape(1, -1)
    return pl.pallas_call(
        _convffn_kernel,
        grid=(s // nt,),
        in_specs=[pl.BlockSpec((nt, bn, d), lambda i: (i, 0, 0)), _const_spec((6, bn, d)), _const_spec((1, d)),
                  _const_spec((d, 2 * d)), _const_spec((1, 2 * d)), _const_spec((CONV_WIDTH, d)),
                  _const_spec((1, d)), _const_spec((1, d)), _const_spec((1, d)),
                  _const_spec((d, d)), _const_spec((1, d)), _const_spec((1, d))] + _ffn_specs(1),
        out_specs=pl.BlockSpec((bn, nt, d), lambda i: (0, i, 0)),
        out_shape=jax.ShapeDtypeStruct((bn, s, d), F32),
        scratch_shapes=[pltpu.VMEM((rows + (CONV_WIDTH - 1) * bn, d), F32), pltpu.VMEM((rows, d), F32),
                        _pad_scratch(bn, d), pltpu.VMEM((rows, d), F32)] + _ffn_scratch(rows, bn),
        compiler_params=_cparams(1),
        name="convffn",
    )(x_tb, mod_t, row(norm_mix_g), pw1_w.astype(BF16), row(pw1_b), dw_w, row(dw_b), row(ln_g), row(ln_b),
      pw2_w.astype(BF16), row(pw2_b), row(norm_ffn_g), *ffn_w)


def kernel(x, c, positions, ada_w, ada_b, norm_mix_g, norm_ffn_g, ab_w_in, a_vnorm_g, a_spatial_w, a_spatial_b, b_q_norm_g, b_k_norm_g, ab_w_out, conv_pw1_w, conv_pw1_b, conv_dw_w, conv_dw_b, conv_ln_g, conv_ln_b, conv_pw2_w, conv_pw2_b, ffn_up_w, ffn_dw_w, ffn_dw_b, ffn_down_w):
    bn, s, d = x.shape
    mod = _ada(c, ada_w, ada_b).reshape(ada_w.shape[0], bn, 6, d)
    cos, sin = _rope_tables(positions)

    ya, qkv = _inproj(x, mod[0], norm_mix_g[0], ab_w_in[0], a_vnorm_g[0], a_spatial_w[0],
                      a_spatial_b[0], b_q_norm_g[0], b_k_norm_g[0], cos, sin)
    branches = []
    for window, dilation in DILATED_PATTERNS:
        assert window // dilation == WINDOW_KEYS
        tq = min(ATTN_QUERY_TILE, s // dilation)
        branches += _attn_branch(*qkv[dilation], tq)
    x = _mix(x, mod[0], ya, branches, ab_w_out[0])
    mod_t = jnp.swapaxes(mod, 1, 2)
    ffn_w = _ffn_weights(ffn_up_w, ffn_dw_w, ffn_dw_b, ffn_down_w)
    x_tb = _ffn0(x, mod_t[0], norm_ffn_g[0], ffn_w)

    return _convffn(x_tb, mod_t[1], norm_mix_g[1], conv_pw1_w[0], conv_pw1_b[0], conv_dw_w[0], conv_dw_b[0],
                    conv_ln_g[0], conv_ln_b[0], conv_pw2_w[0], conv_pw2_b[0], norm_ffn_g[1], ffn_w)
```

```python
import math

import numpy as np
import jax
import jax.numpy as jnp
from jax import lax
from jax.experimental import pallas as pl
from jax.experimental.pallas import tpu as pltpu

F32 = jnp.float32
BF16 = jnp.bfloat16

D_MODEL = 1024
A_WIDTH = 512
A_GROUPS = 4
A_GROUP_DIM = 128
CHUNK = 128
B_WIDTH = 512
B_HEAD_DIM = 64
B_HEADS = 8
HEADS_PER_GROUP = 4
GROUP_LANES = HEADS_PER_GROUP * B_HEAD_DIM
N_GROUPS = B_HEADS // HEADS_PER_GROUP
DILATED_PATTERNS = ((128, 1), (512, 4), (2048, 16))
WINDOW_KEYS = 128
ATTN_QUERY_TILE = 1024
ROPE_THETA = 10000.0
CONV_WIDTH = 31
CONV_ROW_BLOCK = 64
NORM_ROW_BLOCK = 32
GLU_CHUNK = 256
FFN_DIM = 2816
FFN_CONV_WIDTH = 3
FFN_CHUNK = 256
ROPE_ROW_BLOCK = 1024
ADA_COL_BLOCK = 1536
LSE_REP = 16
TB_TOKENS = 64
TB_PITCH = 72
EPS = 1e-6
NEG = -1e30

LANES = 128
SLABS = B_WIDTH // LANES
TOKEN_TILE = 512
V7X_VMEM_LIMIT_BYTES = 56 * 1024 * 1024


def _cparams(n_axes):
    return pltpu.CompilerParams(
        dimension_semantics=("arbitrary",) * n_axes,
        vmem_limit_bytes=V7X_VMEM_LIMIT_BYTES)


def _const_spec(shape):
    nd = len(shape)
    return pl.BlockSpec(shape, lambda *_: (0,) * nd, pipeline_mode=pl.Buffered(1))


def _dot(a, b):
    return jnp.dot(a, b, preferred_element_type=F32)


def _split_dot(a, b2):
    hi = a.astype(BF16)
    lo = (a - hi.astype(F32)).astype(BF16)
    return _dot(jnp.concatenate([hi, lo], axis=1), b2)


def _sigmoid(x):
    return jax.nn.sigmoid(x)


def _gelu(x):
    return 0.5 * x * (1.0 + lax.erf(x * (1.0 / math.sqrt(2.0))))


def _modulate(x, g, shift, scale):
    ms = jnp.mean(x * x, axis=-1, keepdims=True)
    return (x * lax.rsqrt(ms + EPS) * g) * (1.0 + scale) + shift


def _ada_kernel(c_ref, w_ref, b_ref, o_ref):
    c = c_ref[...]
    act = (c * _sigmoid(c)).astype(BF16)
    o_ref[0] = _dot(act, w_ref[0].astype(BF16)) + b_ref[0]


def _ada(c, ada_w, ada_b):
    depth, d, n = ada_w.shape
    bn = c.shape[0]
    nb = ADA_COL_BLOCK
    return pl.pallas_call(
        _ada_kernel,
        grid=(depth, n // nb),
        in_specs=[
            pl.BlockSpec((bn, d), lambda l, j: (0, 0)),
            pl.BlockSpec((1, d, nb), lambda l, j: (l, 0, j)),
            pl.BlockSpec((1, 1, nb), lambda l, j: (l, 0, j)),
        ],
        out_specs=pl.BlockSpec((1, bn, nb), lambda l, j: (l, 0, j)),
        out_shape=jax.ShapeDtypeStruct((depth, bn, n), F32),
        compiler_params=_cparams(2),
        name="ada",
    )(c, ada_w, ada_b.reshape(depth, 1, n))


def _rope_kernel(pos_ref, inv_ref, cos_ref, sin_ref):
    rb = pos_ref.shape[0]
    half = B_HEAD_DIM // 2
    per_row = LANES // half
    ang = pos_ref[...].astype(F32) * inv_ref[...]
    cos, sin = jnp.cos(ang), jnp.sin(ang)
    src = lax.broadcasted_iota(jnp.int32, (2 * LANES, LANES), 0) % LANES
    dst = lax.broadcasted_iota(jnp.int32, (2 * LANES, LANES), 1)
    for j in range(per_row):
        spread = jnp.where((src // half == j) & (src % half == dst % half), 1.0, 0.0).astype(BF16)
        cos_ref[pl.ds(j, rb, stride=per_row), :] = _split_dot(cos, spread)
        sin_ref[pl.ds(j, rb, stride=per_row), :] = _split_dot(sin, spread)


def _rope_tables(positions):
    bn, s = positions.shape
    half = B_HEAD_DIM // 2
    inv_freq = 1.0 / (ROPE_THETA ** (jnp.arange(0, B_HEAD_DIM, 2, dtype=F32) / B_HEAD_DIM))
    per_row = LANES // half
    rows = bn * s // per_row
    pos_rep = jnp.repeat(positions.reshape(rows, per_row), half, axis=1)
    inv = jnp.tile(inv_freq, per_row).reshape(1, LANES)
    rb = min(rows, ROPE_ROW_BLOCK)
    cos, sin = pl.pallas_call(
        _rope_kernel,
        grid=(rows // rb,),
        in_specs=[pl.BlockSpec((rb, LANES), lambda i: (i, 0)),
                  pl.BlockSpec((1, LANES), lambda i: (0, 0))],
        out_specs=[pl.BlockSpec((per_row * rb, LANES), lambda i: (i, 0))] * 2,
        out_shape=[jax.ShapeDtypeStruct((bn * s, LANES), F32)] * 2,
        compiler_params=_cparams(1),
        name="rope",
    )(pos_rep, inv)
    return cos.reshape(bn, s, LANES), sin.reshape(bn, s, LANES)


def _qk_perm():
    half = B_HEAD_DIM // 2
    perm = np.zeros(B_WIDTH, np.int32)
    for grp in range(N_GROUPS):
        for n in range(GROUP_LANES):
            head = grp * HEADS_PER_GROUP + (n % LANES) // half
            perm[grp * GROUP_LANES + n] = head * B_HEAD_DIM + (n // LANES) * half + n % half
    return perm


def _deinterleave(tok_ref, r4_ref, out1_ref, out4_ref, out16_ref):
    tm = tok_ref.shape[1]
    q4, q16 = tm // 4, tm // 16
    for c in range(SLABS):
        lanes = slice(c * LANES, (c + 1) * LANES)
        out1_ref[0, :, lanes] = tok_ref[c].astype(BF16)
        for r in range(4):
            blk = tok_ref[c, pl.ds(r, q4, stride=4), :]
            out4_ref[0, r, :, lanes] = blk.astype(BF16)
            r4_ref[c, r * q4:(r + 1) * q4, :] = blk
        for r in range(4):
            for r2 in range(4):
                blk = r4_ref[c, pl.ds(r * q4 + r2, q16, stride=4), :]
                out16_ref[0, 4 * r2 + r, :, lanes] = blk.astype(BF16)


def _inproj_kernel(x_ref, mod_ref, g_ref, w_ref, vg_ref, ws_ref, wsb_ref, qg_ref, kg_ref,
                   cos_ref, sin_ref, ya_ref, q1_ref, k1_ref, v1_ref, q4_ref, k4_ref, v4_ref,
                   q16_ref, k16_ref, v16_ref, tok_ref, r4_ref):
    tm = x_ref.shape[1]
    x = x_ref[0]
    h = _modulate(x, g_ref[...], mod_ref[0, 0:1, :], mod_ref[0, 1:2, :]).astype(BF16)

    section = lambda k: _dot(h, w_ref[:, k * A_WIDTH:(k + 1) * A_WIDTH])
    z_ua, z_va = section(0), section(1)

    ua = _gelu(z_ua)
    z_q = section(2)
    va = _gelu(z_va)
    row = lax.broadcasted_iota(jnp.int32, (CHUNK, CHUNK), 0)
    col = lax.broadcasted_iota(jnp.int32, (CHUNK, CHUNK), 1)
    causal = row >= col
    vns = []
    for g in range(A_GROUPS):
        lanes = slice(g * A_GROUP_DIM, (g + 1) * A_GROUP_DIM)
        vg = va[:, lanes]
        mu = jnp.mean(vg, axis=-1, keepdims=True)
        var = jnp.mean(jnp.square(vg - mu), axis=-1, keepdims=True)
        vns.append(((vg - mu) * lax.rsqrt(var + EPS) * vg_ref[:, lanes]).astype(BF16))
    z_k = section(3)
    for g in range(A_GROUPS):
        lanes = slice(g * A_GROUP_DIM, (g + 1) * A_GROUP_DIM)
        ws = jnp.where(causal, ws_ref[g], 0.0).astype(BF16)
        bias = wsb_ref[:, g:g + 1]
        for c in range(tm // CHUNK):
            rows = slice(c * CHUNK, (c + 1) * CHUNK)
            f = _dot(ws, vns[g][rows, :]) + bias
            ya_ref[0, rows, lanes] = (ua[rows, lanes] * f).astype(BF16)
    z_v = section(4)

    seg_r = lax.broadcasted_iota(jnp.int32, (2 * LANES, LANES), 0) % LANES // (B_HEAD_DIM // 2)
    seg_c = lax.broadcasted_iota(jnp.int32, (2 * LANES, LANES), 1) // (B_HEAD_DIM // 2)
    seg_ones = jnp.where(seg_r == seg_c, 1.0, 0.0).astype(BF16)
    cos = cos_ref[0]
    sin = sin_ref[0]

    def norm_rot(z, gain_ref, tok, out_scale):
        for grp in range(N_GROUPS):
            lo = grp * GROUP_LANES
            z1 = z[:, lo:lo + LANES]
            z2 = z[:, lo + LANES:lo + 2 * LANES]
            ss = _split_dot(z1 * z1 + z2 * z2, seg_ones)
            inv = lax.rsqrt(ss * (1.0 / B_HEAD_DIM) + EPS)
            a1 = z1 * inv * gain_ref[:, lo:lo + LANES]
            a2 = z2 * inv * gain_ref[:, lo + LANES:lo + 2 * LANES]
            tok[2 * grp] = (a1 * cos - a2 * sin) * out_scale
            tok[2 * grp + 1] = (a2 * cos + a1 * sin) * out_scale

    norm_rot(z_q, qg_ref, tok_ref.at[0], B_HEAD_DIM ** -0.5)
    _deinterleave(tok_ref.at[0], r4_ref.at[0], q1_ref, q4_ref, q16_ref)
    norm_rot(z_k, kg_ref, tok_ref.at[1], 1.0)
    _deinterleave(tok_ref.at[1], r4_ref.at[1], k1_ref, k4_ref, k16_ref)
    for c in range(SLABS):
        tok_ref[2, c] = z_v[:, c * LANES:(c + 1) * LANES]
    _deinterleave(tok_ref.at[2], r4_ref.at[2], v1_ref, v4_ref, v16_ref)


def _inproj(x, mod, norm_g, w_in, a_vnorm_g, a_spatial_w, a_spatial_b, q_g, k_g, cos, sin):
    bn, s, d = x.shape
    tm = TOKEN_TILE
    perm = _qk_perm()
    qs = 2 * A_WIDTH
    w = jnp.concatenate([w_in[:, :qs], w_in[:, qs:qs + B_WIDTH][:, perm],
                         w_in[:, qs + B_WIDTH:qs + 2 * B_WIDTH][:, perm],
                         w_in[:, qs + 2 * B_WIDTH:]], axis=1).astype(BF16)
    half = B_HEAD_DIM // 2

    def gain_row(g):
        grp = jnp.concatenate([jnp.tile(g[:half], HEADS_PER_GROUP), jnp.tile(g[half:], HEADS_PER_GROUP)])
        return jnp.tile(grp, N_GROUPS).reshape(1, B_WIDTH)

    tok = lambda width: pl.BlockSpec((1, tm, width), lambda b, i: (b, i, 0))
    res = lambda dil: pl.BlockSpec((1, dil, tm // dil, B_WIDTH), lambda b, i: (b, 0, i, 0))
    sds1 = jax.ShapeDtypeStruct((bn, s, B_WIDTH), BF16)
    sds = lambda dil: jax.ShapeDtypeStruct((bn, dil, s // dil, B_WIDTH), BF16)
    outs = pl.pallas_call(
        _inproj_kernel,
        grid=(bn, s // tm),
        in_specs=[
            tok(d),
            pl.BlockSpec((1, 6, d), lambda b, i: (b, 0, 0)),
            _const_spec((1, d)),
            _const_spec(w.shape),
            _const_spec((1, A_WIDTH)),
            _const_spec((A_GROUPS, CHUNK, CHUNK)),
            _const_spec((CHUNK, A_GROUPS)),
            _const_spec((1, B_WIDTH)),
            _const_spec((1, B_WIDTH)),
            tok(LANES),
            tok(LANES),
        ],
        out_specs=[tok(A_WIDTH)] + [tok(B_WIDTH)] * 3 + [res(4)] * 3 + [res(16)] * 3,
        out_shape=[jax.ShapeDtypeStruct((bn, s, A_WIDTH), BF16)] + [sds1] * 3 + [sds(4)] * 3 + [sds(16)] * 3,
        scratch_shapes=[pltpu.VMEM((3, SLABS, tm, LANES), F32), pltpu.VMEM((3, SLABS, tm, LANES), F32)],
        compiler_params=_cparams(2),
        name="inproj",
    )(x, mod, norm_g.reshape(1, d), w, a_vnorm_g.reshape(1, A_WIDTH), a_spatial_w,
      a_spatial_b.T, gain_row(q_g), gain_row(k_g), cos, sin)
    ya = outs[0]
    qkv = {1: [t.reshape(bn, 1, s, B_WIDTH) for t in outs[1:4]], 4: outs[4:7], 16: outs[7:10]}
    return ya, qkv


def _attn_kernel(qmask_ref, vmask_ref, q_ref, kp_ref, kc_ref, vp_ref, vc_ref, o_ref, lse_ref):
    tq = q_ref.shape[0]
    nsub = tq // WINDOW_KEYS
    n = pl.program_id(2)
    qi = lax.broadcasted_iota(jnp.int32, (WINDOW_KEYS, 2 * WINDOW_KEYS), 0)
    kj = lax.broadcasted_iota(jnp.int32, (WINDOW_KEYS, 2 * WINDOW_KEYS), 1)
    dist = qi + WINDOW_KEYS - kj
    band = (dist >= 0) & (dist <= WINDOW_KEYS)
    first_mask = band & ((kj >= WINDOW_KEYS) | (n > 0))
    vlane = lax.broadcasted_iota(jnp.int32, (WINDOW_KEYS, GROUP_LANES), 1) // B_HEAD_DIM
    lse_pick = (lax.broadcasted_iota(jnp.int32, (WINDOW_KEYS, LANES), 1) % B_HEAD_DIM) // LSE_REP

    def scores(i):
        rows = slice(i * WINDOW_KEYS, (i + 1) * WINDOW_KEYS)
        k_prev = kp_ref[...] if i == 0 else kc_ref[(i - 1) * WINDOW_KEYS:i * WINDOW_KEYS, :]
        kk = jnp.concatenate([k_prev, kc_ref[rows, :]], axis=0)
        q = q_ref[rows, :]
        mask = first_mask if i == 0 else band
        out = []
        for grp in range(N_GROUPS):
            lanes = slice(grp * GROUP_LANES, (grp + 1) * GROUP_LANES)
            qg, kg = q[:, lanes], kk[:, lanes]
            for hh in range(HEADS_PER_GROUP):
                qm = jnp.where(qmask_ref[hh] > 0, qg, jnp.zeros_like(qg))
                sc = lax.dot_general(qm, kg, (((1,), (1,)), ((), ())), preferred_element_type=F32)
                out.append(jnp.where(mask, sc, NEG))
        return out

    def per_head(block):
        out = []
        for grp in range(N_GROUPS):
            vg = block[:, grp * GROUP_LANES:(grp + 1) * GROUP_LANES]
            out.append([jnp.where(vmask_ref[2 * hh * WINDOW_KEYS:(2 * hh + 1) * WINDOW_KEYS, :] > 0, vg,
                                  jnp.zeros_like(vg)) for hh in range(HEADS_PER_GROUP)])
        return out

    def finish(i, scs, v_prev):
        rows = slice(i * WINDOW_KEYS, (i + 1) * WINDOW_KEYS)
        v_cur = per_head(vc_ref[rows, :])
        head_sel = vmask_ref[...]
        lse_parts = []
        for grp in range(N_GROUPS):
            lanes = slice(grp * GROUP_LANES, (grp + 1) * GROUP_LANES)
            v_bd = jnp.concatenate([blk[grp][hh] for hh in range(HEADS_PER_GROUP) for blk in (v_prev, v_cur)],
                                   axis=0)
            ps, ms = [], []
            for hh in range(HEADS_PER_GROUP):
                sc = scs[grp * HEADS_PER_GROUP + hh]
                m = jnp.max(sc, axis=-1, keepdims=True)
                ps.append(jnp.exp(sc - m).astype(BF16))
                ms.append(m)
            p_cat = jnp.concatenate(ps, axis=1)
            pv = _dot(p_cat, v_bd)
            den = _dot(p_cat, head_sel)
            m_cat = jnp.broadcast_to(ms[-1], (WINDOW_KEYS, GROUP_LANES))
            for hh in range(HEADS_PER_GROUP - 1):
                m_cat = jnp.where(vlane == hh, ms[hh], m_cat)
            o_ref[rows, lanes] = (pv / den).astype(BF16)
            lse = m_cat + jnp.log(den)
            lse_parts += [lse[:, :LANES], lse[:, LANES:]]
        tile = lse_parts[-1]
        for part in range(len(lse_parts) - 1):
            tile = jnp.where(lse_pick == part, lse_parts[part], tile)
        lse_ref[rows, :] = tile
        return v_cur

    pending = scores(0)
    v_prev = per_head(vp_ref[...])
    for i in range(nsub):
        nxt = scores(i + 1) if i + 1 < nsub else None
        v_prev = finish(i, pending, v_prev)
        pending = nxt


def _attn_branch(q, k, v, tq):
    bn, dilation, sub_len, w = q.shape
    blocks_per_step = tq // WINDOW_KEYS
    cur = pl.BlockSpec((None, None, tq, w), lambda b, r, n: (b, r, n, 0))
    prev = pl.BlockSpec((None, None, WINDOW_KEYS, w),
                        lambda b, r, n: (b, r, jnp.maximum(n * blocks_per_step - 1, 0), 0))
    half = B_HEAD_DIM // 2
    lane = np.arange(GROUP_LANES)
    qmask = np.stack([np.broadcast_to(((lane % LANES) // half == hh), (WINDOW_KEYS, GROUP_LANES))
                      for hh in range(HEADS_PER_GROUP)]).astype(np.float32)
    vmask = np.concatenate([np.broadcast_to((lane // B_HEAD_DIM == hh), (2 * WINDOW_KEYS, GROUP_LANES))
                            for hh in range(HEADS_PER_GROUP)]).astype(np.float32)
    return pl.pallas_call(
        _attn_kernel,
        grid=(bn, dilation, sub_len // tq),
        in_specs=[_const_spec(qmask.shape), _const_spec(vmask.shape), cur, prev, cur, prev, cur],
        out_specs=[cur, pl.BlockSpec((None, None, tq, LANES), lambda b, r, n: (b, r, n, 0))],
        out_shape=[jax.ShapeDtypeStruct(q.shape, BF16),
                   jax.ShapeDtypeStruct((bn, dilation, sub_len, LANES), F32)],
        compiler_params=_cparams(3),
        name=f"attn_d{dilation}",
    )(jnp.asarray(qmask, BF16), jnp.asarray(vmask, BF16), q, k, k, v, v)


def _to_token_batch(x_ref, pad_ref, tb_ref):
    nb, nt, d = x_ref.shape
    for c in range(d // LANES):
        lanes = slice(c * LANES, (c + 1) * LANES)
        for b in range(nb):
            pad_ref[c, b * TB_PITCH:b * TB_PITCH + nt, :] = x_ref[b, :, lanes]
        for t in range(nt):
            tb_ref[t * nb:(t + 1) * nb, lanes] = pad_ref[c, pl.ds(t, nb, stride=TB_PITCH), :]


def _from_token_batch(tb_ref, pad_ref, out_ref):
    nb, nt, d = out_ref.shape
    for c in range(d // LANES):
        lanes = slice(c * LANES, (c + 1) * LANES)
        for t in range(nt):
            pad_ref[c, pl.ds(t, nb, stride=TB_PITCH), :] = tb_ref[t * nb:(t + 1) * nb, lanes]
        for b in range(nb):
            out_ref[b, :, lanes] = pad_ref[c, b * TB_PITCH:b * TB_PITCH + nt, :]


def _ffn(x1, nb, mod_ref, g_ref, up_ref, dww_ref, dwb_ref, down_ref, act_ref, carry_ref):
    rows = x1.shape[0]
    rep = lambda k: jnp.tile(mod_ref[k], (rows // nb, 1))
    h = _modulate(x1, g_ref[...], rep(3), rep(4)).astype(BF16)

    def up(c):
        return (_dot(h, up_ref[:, c * FFN_CHUNK:(c + 1) * FFN_CHUNK]),
                _dot(h, up_ref[:, FFN_DIM + c * FFN_CHUNK:FFN_DIM + (c + 1) * FFN_CHUNK]))

    def conv(z, col0):
        cols = slice(col0, col0 + FFN_CHUNK)
        hist = carry_ref[:, cols]
        carry_ref[:, cols] = z[rows - 2 * nb:, :]
        z1 = jnp.concatenate([hist[nb:], z[:rows - nb]], axis=0)
        z2 = jnp.concatenate([hist, z[:rows - 2 * nb]], axis=0)
        return (z * dww_ref[2:3, cols] + z1 * dww_ref[1:2, cols] + z2 * dww_ref[0:1, cols]
                + dwb_ref[:, cols])

    nch = FFN_DIM // FFN_CHUNK
    nxt = up(0)
    for c in range(nch):
        za, zb = nxt
        if c + 1 < nch:
            nxt = up(c + 1)
        a = conv(za, c * FFN_CHUNK)
        b = conv(zb, FFN_DIM + c * FFN_CHUNK)
        act_ref[:, c * FFN_CHUNK:(c + 1) * FFN_CHUNK] = (a * _sigmoid(a) * b).astype(BF16)
    return x1 + rep(5) * _dot(act_ref[...], down_ref[...])


def _ffn_weights(up_w, dw_w, dw_b, down_w):
    return up_w.astype(BF16), dw_w, dw_b.reshape(-1, 1, 2 * FFN_DIM), down_w.astype(BF16)


def _ffn_specs(layer):
    pick = lambda *shape: pl.BlockSpec((None,) + shape, lambda *_: (layer, 0, 0), pipeline_mode=pl.Buffered(1))
    return [pick(D_MODEL, 2 * FFN_DIM), pick(FFN_CONV_WIDTH, 2 * FFN_DIM), pick(1, 2 * FFN_DIM),
            pick(FFN_DIM, D_MODEL)]


def _ffn_scratch(rows, nb):
    return [pltpu.VMEM((rows, FFN_DIM), BF16), pltpu.VMEM(((FFN_CONV_WIDTH - 1) * nb, 2 * FFN_DIM), F32)]


def _pad_scratch(nb, d):
    return pltpu.VMEM((d // LANES, nb * TB_PITCH, LANES), F32)


def _ffn0_kernel(x_ref, mod_ref, g_ref, up_ref, dww_ref, dwb_ref, down_ref, out_ref,
                 pad_ref, tb_ref, act_ref, carry_ref):
    nb = x_ref.shape[0]

    @pl.when(pl.program_id(0) == 0)
    def _():
        carry_ref[...] = jnp.zeros_like(carry_ref)

    _to_token_batch(x_ref, pad_ref, tb_ref)
    y = _ffn(tb_ref[...], nb, mod_ref, g_ref, up_ref, dww_ref, dwb_ref, down_ref, act_ref, carry_ref)
    out_ref[...] = y.reshape(out_ref.shape)


def _ffn0(x, mod_t, norm_g, ffn_w):
    bn, s, d = x.shape
    nt = TB_TOKENS
    rows = nt * bn
    return pl.pallas_call(
        _ffn0_kernel,
        grid=(s // nt,),
        in_specs=[pl.BlockSpec((bn, nt, d), lambda i: (0, i, 0)), _const_spec((6, bn, d)),
                  _const_spec((1, d))] + _ffn_specs(0),
        out_specs=pl.BlockSpec((nt, bn, d), lambda i: (i, 0, 0)),
        out_shape=jax.ShapeDtypeStruct((s, bn, d), F32),
        scratch_shapes=[_pad_scratch(bn, d), pltpu.VMEM((rows, d), F32)] + _ffn_scratch(rows, bn),
        compiler_params=_cparams(1),
        name="ffn0",
    )(x, mod_t, norm_g.reshape(1, d), *ffn_w)


def _interleave4(src_ref, dst_ref):
    q4 = src_ref.shape[1]
    for c in range(src_ref.shape[2] // LANES):
        for r in range(4):
            dst_ref[c, pl.ds(r, q4, stride=4), :] = src_ref[r, :, c * LANES:(c + 1) * LANES].astype(F32)


def _interleave16(src_ref, mid_ref, dst_ref):
    q16 = src_ref.shape[1]
    q4 = 4 * q16
    for c in range(src_ref.shape[2] // LANES):
        for r in range(4):
            for r2 in range(4):
                mid_ref[c, pl.ds(r * q4 + r2, q16, stride=4), :] = (
                    src_ref[4 * r2 + r, :, c * LANES:(c + 1) * LANES].astype(F32))
        for r in range(4):
            dst_ref[c, pl.ds(r, q4, stride=4), :] = mid_ref[c, r * q4:(r + 1) * q4, :]


def _slabs(ref):
    return jnp.concatenate([ref[c] for c in range(ref.shape[0])], axis=1)


def _lse_lane(head):
    return B_HEAD_DIM * (head % 2) + LSE_REP * (head // 2)


def _mix_kernel(x_ref, mod_ref, ya_ref, o1_ref, l1_ref, o4_ref, l4_ref, o16_ref, l16_ref,
                exp_ref, wout_ref, out_ref, ilo_ref, mido_ref, ill_ref, midl_ref):
    _interleave4(o4_ref, ilo_ref.at[0])
    _interleave16(o16_ref, mido_ref, ilo_ref.at[1])
    _interleave4(l4_ref, ill_ref.at[0])
    _interleave16(l16_ref, midl_ref, ill_ref.at[1])
    l1, l2, l3 = l1_ref[...], ill_ref[0, 0], ill_ref[1, 0]
    mx = jnp.maximum(l1, jnp.maximum(l2, l3))
    e1, e2, e3 = jnp.exp(l1 - mx), jnp.exp(l2 - mx), jnp.exp(l3 - mx)
    tot = e1 + e2 + e3
    expand = exp_ref[...]
    yb = (_split_dot(e1 / tot, expand) * o1_ref[...].astype(F32)
          + _split_dot(e2 / tot, expand) * _slabs(ilo_ref.at[0])
          + _split_dot(e3 / tot, expand) * _slabs(ilo_ref.at[1]))
    y = _dot(ya_ref[0], wout_ref[0:A_WIDTH, :]) + _dot(yb.astype(BF16), wout_ref[A_WIDTH:, :])
    out_ref[0] = x_ref[0] + mod_ref[0, 2:3, :] * y


def _mix(x, mod, ya, branches, w_out):
    bn, s, d = x.shape
    tm = TOKEN_TILE
    tok = lambda width: pl.BlockSpec((1, tm, width), lambda b, i: (b, i, 0))
    branch_specs = []
    for dil in (1, 4, 16):
        for width in (B_WIDTH, LANES):
            if dil == 1:
                branch_specs.append(pl.BlockSpec((None, None, tm, width), lambda b, i: (b, 0, i, 0)))
            else:
                branch_specs.append(pl.BlockSpec((None, dil, tm // dil, width), lambda b, i: (b, 0, i, 0)))
    expand = np.zeros((2 * LANES, B_WIDTH), np.float32)
    for head in range(B_HEADS):
        expand[_lse_lane(head)::LANES, head * B_HEAD_DIM:(head + 1) * B_HEAD_DIM] = 1.0
    return pl.pallas_call(
        _mix_kernel,
        grid=(bn, s // tm),
        in_specs=[tok(d), pl.BlockSpec((1, 6, d), lambda b, i: (b, 0, 0)), tok(A_WIDTH)] + branch_specs
                 + [_const_spec((2 * LANES, B_WIDTH)), _const_spec((A_WIDTH + B_WIDTH, d))],
        out_specs=tok(d),
        out_shape=jax.ShapeDtypeStruct((bn, s, d), F32),
        scratch_shapes=[pltpu.VMEM((2, SLABS, tm, LANES), F32), pltpu.VMEM((SLABS, tm, LANES), F32),
                        pltpu.VMEM((2, 1, tm, LANES), F32), pltpu.VMEM((1, tm, LANES), F32)],
        compiler_params=_cparams(2),
        name="mix",
    )(x, mod, ya, *branches, jnp.asarray(expand, BF16), w_out.astype(BF16))


def _convffn_kernel(x_ref, mod_ref, gm_ref, pw1_ref, pw1b_ref, dw_ref, dwb_ref, lng_ref, lnb_ref,
                    pw2_ref, pw2b_ref, g_ref, up_ref, fdww_ref, fdwb_ref, down_ref, out_ref,
                    ybuf_ref, cv_ref, pad_ref, tb_ref, act_ref, carry_ref):
    nt, nb, d = x_ref.shape
    rows = nt * nb
    hist = (CONV_WIDTH - 1) * nb

    @pl.when(pl.program_id(0) == 0)
    def _():
        carry_ref[...] = jnp.zeros_like(carry_ref)
        ybuf_ref[0:hist, :] = jnp.zeros((hist, d), F32)

    rep = lambda k: jnp.tile(mod_ref[k], (nt, 1))
    x = x_ref[...].reshape(rows, d)
    h = _modulate(x, gm_ref[...], rep(0), rep(1)).astype(BF16)

    def pw1(j):
        cols = slice(j * GLU_CHUNK, (j + 1) * GLU_CHUNK)
        gcols = slice(d + j * GLU_CHUNK, d + (j + 1) * GLU_CHUNK)
        return (_dot(h, pw1_ref[:, cols]) + pw1b_ref[:, cols],
                _dot(h, pw1_ref[:, gcols]) + pw1b_ref[:, gcols])

    rb = CONV_ROW_BLOCK
    n_glu = d // GLU_CHUNK
    nxt = pw1(0)
    for j in range(n_glu):
        a, gate = nxt
        if j + 1 < n_glu:
            nxt = pw1(j + 1)
        lanes = slice(j * GLU_CHUNK, (j + 1) * GLU_CHUNK)
        ybuf_ref[hist:hist + rows, lanes] = a * _sigmoid(gate)
        for r in range(rows // rb):
            acc = jnp.zeros((rb, GLU_CHUNK), F32) + dwb_ref[:, lanes]
            for k in range(CONV_WIDTH):
                start = r * rb + k * nb
                acc = acc + ybuf_ref[start:start + rb, lanes] * dw_ref[k:k + 1, lanes]
            cv_ref[r * rb:(r + 1) * rb, lanes] = acc
        ybuf_ref[0:hist, lanes] = ybuf_ref[rows:rows + hist, lanes]

    for r in range(rows // NORM_ROW_BLOCK):
        blk = slice(r * NORM_ROW_BLOCK, (r + 1) * NORM_ROW_BLOCK)
        cv = cv_ref[blk, :]
        mu = jnp.mean(cv, axis=-1, keepdims=True)
        var = jnp.mean(jnp.square(cv - mu), axis=-1, keepdims=True)
        yn = (cv - mu) * lax.rsqrt(var + EPS) * lng_ref[...] + lnb_ref[...]
        act_ref[blk, 0:d] = (yn * _sigmoid(yn)).astype(BF16)
    y = _dot(act_ref[:, 0:d], pw2_ref[...]) + pw2b_ref[...]
    x1 = x + rep(2) * y
    tb_ref[...] = _ffn(x1, nb, mod_ref, g_ref, up_ref, fdww_ref, fdwb_ref, down_ref, act_ref, carry_ref)
    _from_token_batch(tb_ref, pad_ref, out_ref)


def _convffn(x_tb, mod_t, norm_mix_g, pw1_w, pw1_b, dw_w, dw_b, ln_g, ln_b, pw2_w, pw2_b, norm_ffn_g, ffn_w):
    s, bn, d = x_tb.shape
    nt = TB_TOKENS
    rows = nt * bn
    row = lambda t: t.reshape(1, -1)
    return pl.pallas_call(
        _convffn_kernel,
        grid=(s // nt,),
        in_specs=[pl.BlockSpec((nt, bn, d), lambda i: (i, 0, 0)), _const_spec((6, bn, d)), _const_spec((1, d)),
                  _const_spec((d, 2 * d)), _const_spec((1, 2 * d)), _const_spec((CONV_WIDTH, d)),
                  _const_spec((1, d)), _const_spec((1, d)), _const_spec((1, d)),
                  _const_spec((d, d)), _const_spec((1, d)), _const_spec((1, d))] + _ffn_specs(1),
        out_specs=pl.BlockSpec((bn, nt, d), lambda i: (0, i, 0)),
        out_shape=jax.ShapeDtypeStruct((bn, s, d), F32),
        scratch_shapes=[pltpu.VMEM((rows + (CONV_WIDTH - 1) * bn, d), F32), pltpu.VMEM((rows, d), F32),
                        _pad_scratch(bn, d), pltpu.VMEM((rows, d), F32)] + _ffn_scratch(rows, bn),
        compiler_params=_cparams(1),
        name="convffn",
    )(x_tb, mod_t, row(norm_mix_g), pw1_w.astype(BF16), row(pw1_b), dw_w, row(dw_b), row(ln_g), row(ln_b),
      pw2_w.astype(BF16), row(pw2_b), row(norm_ffn_g), *ffn_w)


def kernel(x, c, positions, ada_w, ada_b, norm_mix_g, norm_ffn_g, ab_w_in, a_vnorm_g, a_spatial_w, a_spatial_b, b_q_norm_g, b_k_norm_g, ab_w_out, conv_pw1_w, conv_pw1_b, conv_dw_w, conv_dw_b, conv_ln_g, conv_ln_b, conv_pw2_w, conv_pw2_b, ffn_up_w, ffn_dw_w, ffn_dw_b, ffn_down_w):
    bn, s, d = x.shape
    mod = _ada(c, ada_w, ada_b).reshape(ada_w.shape[0], bn, 6, d)
    cos, sin = _rope_tables(positions)

    ya, qkv = _inproj(x, mod[0], norm_mix_g[0], ab_w_in[0], a_vnorm_g[0], a_spatial_w[0],
                      a_spatial_b[0], b_q_norm_g[0], b_k_norm_g[0], cos, sin)
    branches = []
    for window, dilation in DILATED_PATTERNS:
        assert window // dilation == WINDOW_KEYS
        tq = min(ATTN_QUERY_TILE, s // dilation)
        branches += _attn_branch(*qkv[dilation], tq)
    x = _mix(x, mod[0], ya, branches, ab_w_out[0])
    mod_t = jnp.swapaxes(mod, 1, 2)
    ffn_w = _ffn_weights(ffn_up_w, ffn_dw_w, ffn_dw_b, ffn_down_w)
    x_tb = _ffn0(x, mod_t[0], norm_ffn_g[0], ffn_w)

    return _convffn(x_tb, mod_t[1], norm_mix_g[1], conv_pw1_w[0], conv_pw1_b[0], conv_dw_w[0], conv_dw_b[0],
                    conv_ln_g[0], conv_ln_b[0], conv_pw2_w[0], conv_pw2_b[0], norm_ffn_g[1], ffn_w)
```
